```python
import math, functools
import jax
import jax.numpy as jnp
from jax import lax
import numpy as np

D_MODEL = 1024
BATCH = 2
SEQ = 16384
DEPTH = 2
DEC_BATCH = 8
DEC_SEQ = 4096
PAST_LEN = 128

N_HEADS = 4
HEAD_DIM = 128
MIX_WIDTH = N_HEADS * HEAD_DIM
N_BRANCH = 3
CHUNK = 64
CONV_K = 5
D_FF = 4 * D_MODEL
EPS = 1e-6
NEG_BIG = -1e30
IN_LAYOUT = (
    MIX_WIDTH, MIX_WIDTH, MIX_WIDTH, MIX_WIDTH, 2 * N_HEADS, 2 * N_HEADS,
    MIX_WIDTH, 2 * MIX_WIDTH, MIX_WIDTH, MIX_WIDTH,
    MIX_WIDTH, MIX_WIDTH, MIX_WIDTH, MIX_WIDTH, 2 * N_HEADS, 2 * N_HEADS,
    N_BRANCH * D_MODEL,
)
IN_COLS = 13 * MIX_WIDTH + 8 * N_HEADS + N_BRANCH * D_MODEL

kernel_name = 'hybrid_bidir_deltanet_hgrn2_mlstm_encoder'


def rmsnorm(x, g):
    xf = x.astype(jnp.float32)
    y = xf * lax.rsqrt(jnp.mean(xf * xf, axis=-1, keepdims=True) + EPS)
    return (y * g.astype(jnp.float32)).astype(x.dtype)


def l2norm(x):
    return x * lax.rsqrt(jnp.sum(x * x, axis=-1, keepdims=True) + EPS)


def split_cols(z, widths):
    parts, off = [], 0
    for w in widths:
        parts.append(z[..., off:off + w])
        off += w
    return parts


def heads(t):
    return t.reshape(t.shape[:-1] + (N_HEADS, HEAD_DIM))


def dir_pairs(t):
    return t.reshape(t.shape[:-1] + (2, t.shape[-1] // 2))


def flip(t):
    return jnp.flip(t, axis=1)


def centred_dwconv(x, w):
    c = x.shape[-1]
    return lax.conv_general_dilated(
        x, w.astype(x.dtype)[:, None, :], window_strides=(1,),
        padding=[(CONV_K // 2, CONV_K // 2)],
        dimension_numbers=('NWC', 'WIO', 'NWC'), feature_group_count=c)


def to_chunks(x):
    b, s = x.shape[0], x.shape[1]
    x = x.reshape((b, s // CHUNK, CHUNK) + x.shape[2:])
    return jnp.swapaxes(jnp.moveaxis(x, 1, 0), 2, 3)


def from_chunks(x):
    x = jnp.moveaxis(jnp.swapaxes(x, 2, 3), 0, 1)
    return x.reshape((x.shape[0], x.shape[1] * x.shape[2]) + x.shape[3:])


def gated_delta_chunked(q, k, v, beta, g):
    bsz, _, nh, dk = k.shape
    dv = v.shape[-1]
    qc, kc, vc = to_chunks(q), to_chunks(k), to_chunks(v)
    bc = to_chunks(beta)
    gcum = jnp.cumsum(to_chunks(g), axis=-1)
    idx = jnp.arange(CHUNK)
    incl = idx[:, None] >= idx[None, :]
    strict = idx[:, None] > idx[None, :]
    decay = jnp.exp(jnp.where(incl, gcum[..., :, None] - gcum[..., None, :], NEG_BIG))
    kb = kc * bc[..., None]
    a_low = jnp.where(strict, jnp.einsum('nbhid,nbhjd->nbhij', kb, kc) * decay, 0.0)
    rhs = jnp.concatenate([vc * bc[..., None], kb * jnp.exp(gcum)[..., None]], axis=-1)
    sol = lax.linalg.triangular_solve(a_low, rhs, left_side=True, lower=True, unit_diagonal=True)
    u, w = sol[..., :dv], sol[..., dv:]
    attn = jnp.einsum('nbhid,nbhjd->nbhij', qc, kc) * decay
    q_dec = qc * jnp.exp(gcum)[..., None]
    k_dec = kc * jnp.exp(gcum[..., -1:] - gcum)[..., None]
    a_last = jnp.exp(gcum[..., -1])

    def step(state, xs):
        u_n, w_n, attn_n, qd_n, kd_n, al_n = xs
        v_new = u_n - jnp.einsum('bhcd,bhde->bhce', w_n, state)
        out = jnp.einsum('bhcd,bhde->bhce', qd_n, state) + jnp.einsum('bhij,bhje->bhie', attn_n, v_new)
        state = state * al_n[..., None, None] + jnp.einsum('bhcd,bhce->bhde', kd_n, v_new)
        return state, out

    s0 = jnp.zeros((bsz, nh, dk, dv), jnp.float32)
    _, o = lax.scan(step, s0, (u, w, attn, q_dec, k_dec, a_last))
    return from_chunks(o)


def gla_chunked(q, k, v, logf):
    bsz, _, nh, dk = k.shape
    dv = v.shape[-1]
    qc, kc, vc = to_chunks(q), to_chunks(k), to_chunks(v)
    bcum = jnp.cumsum(to_chunks(logf), axis=-2)
    idx = jnp.arange(CHUNK)
    incl3 = (idx[:, None] >= idx[None, :])[:, :, None]

    def step(state, xs):
        q_n, k_n, v_n, b_n = xs
        dec = jnp.exp(jnp.where(incl3, b_n[..., :, None, :] - b_n[..., None, :, :], NEG_BIG))
        attn = jnp.einsum('bhijd,bhjd->bhij', q_n[..., :, None, :] * dec, k_n)
        out = jnp.einsum('bhcd,bhde->bhce', q_n * jnp.exp(b_n), state) + jnp.einsum('bhij,bhje->bhie', attn, v_n)
        b_last = b_n[..., -1, :]
        state = state * jnp.exp(b_last)[..., None] + jnp.einsum(
            'bhcd,bhce->bhde', k_n * jnp.exp(b_last[..., None, :] - b_n), v_n)
        return state, out

    s0 = jnp.zeros((bsz, nh, dk, dv), jnp.float32)
    _, o = lax.scan(step, s0, (qc, kc, vc, bcum))
    return from_chunks(o)


def mlstm_chunked(q, k, v, i_pre, logf):
    bsz, _, nh, dk = k.shape
    dv = v.shape[-1]
    qc, kc, vc = to_chunks(q), to_chunks(k), to_chunks(v)
    ic = to_chunks(i_pre)
    bcum = jnp.cumsum(to_chunks(logf), axis=-1)
    idx = jnp.arange(CHUNK)
    incl = idx[:, None] >= idx[None, :]

    def step(carry, xs):
        cmat, nvec, m = carry
        q_n, k_n, v_n, i_n, b_n = xs
        logw = jnp.where(incl, b_n[..., :, None] - b_n[..., None, :] + i_n[..., None, :], NEG_BIG)
        inter = b_n + m[..., None]
        m_row = jnp.maximum(inter, jnp.max(logw, axis=-1))
        wmat = jnp.exp(logw - m_row[..., None])
        s_inter = jnp.exp(inter - m_row)
        qk = jnp.einsum('bhid,bhjd->bhij', q_n, k_n) * wmat
        num = s_inter[..., None] * jnp.einsum('bhid,bhde->bhie', q_n, cmat) + jnp.einsum('bhij,bhje->bhie', qk, v_n)
        den = s_inter * jnp.einsum('bhid,bhd->bhi', q_n, nvec) + jnp.sum(qk, axis=-1)
        h = num / jnp.maximum(jnp.abs(den), jnp.exp(-m_row))[..., None]
        b_last = b_n[..., -1]
        logw_end = b_last[..., None] - b_n + i_n
        m_new = jnp.maximum(b_last + m, jnp.max(logw_end, axis=-1))
        w_end = jnp.exp(logw_end - m_new[..., None])
        s_old = jnp.exp(b_last + m - m_new)
        cmat = s_old[..., None, None] * cmat + jnp.einsum('bhc,bhcd,bhce->bhde', w_end, k_n, v_n)
        nvec = s_old[..., None] * nvec + jnp.einsum('bhc,bhcd->bhd', w_end, k_n)
        return (cmat, nvec, m_new), h

    init = (jnp.zeros((bsz, nh, dk, dv), jnp.float32),
            jnp.zeros((bsz, nh, dk), jnp.float32),
            jnp.full((bsz, nh), NEG_BIG, jnp.float32))
    _, o = lax.scan(step, init, (qc, kc, vc, ic, bcum))
    return from_chunks(o)


def deltanet_branch(q, k, v, z, beta_pre, alpha_pre, conv_w, a_log, dt_bias, norm_w):
    qkv = jax.nn.silu(centred_dwconv(jnp.concatenate([q, k, v], axis=-1), conv_w))
    q, k, v = (heads(t) for t in jnp.split(qkv, 3, axis=-1))
    q = l2norm(q) * (HEAD_DIM ** -0.5)
    k = l2norm(k)
    beta = jax.nn.sigmoid(dir_pairs(beta_pre))
    g = -jnp.exp(a_log.astype(jnp.float32)) * jax.nn.softplus(dir_pairs(alpha_pre) + dt_bias.astype(jnp.float32))
    o = (gated_delta_chunked(q, k, v, beta[:, :, 0], g[:, :, 0])
         + flip(gated_delta_chunked(flip(q), flip(k), flip(v), flip(beta[:, :, 1]), flip(g[:, :, 1]))))
    o = rmsnorm(o, norm_w) * jax.nn.silu(heads(z))
    return o.reshape(o.shape[:2] + (MIX_WIDTH,))


def hgrn2_branch(q, f_pre, i, z, lb, norm_w):
    q = jax.nn.silu(heads(q))
    i = heads(i)
    f_pre = heads(dir_pairs(f_pre))
    lb = lb.astype(jnp.float32).reshape(2, N_HEADS, HEAD_DIM)
    k = (1.0 - lb) * jax.nn.sigmoid(-f_pre)
    logf = jnp.log(lb + (1.0 - lb) * jax.nn.sigmoid(f_pre))
    o = (gla_chunked(q, k[:, :, 0], i, logf[:, :, 0])
         + flip(gla_chunked(flip(q), flip(k[:, :, 1]), flip(i), flip(logf[:, :, 1]))))
    o = rmsnorm(o, norm_w) * jax.nn.silu(heads(z))
    return o.reshape(o.shape[:2] + (MIX_WIDTH,))


def mlstm_branch(q, k, v, o_pre, i_pre, f_pre, conv_w, i_bias, f_bias, norm_w):
    qk = jax.nn.silu(centred_dwconv(jnp.concatenate([q, k], axis=-1), conv_w))
    q, k = (heads(t) for t in jnp.split(qk, 2, axis=-1))
    q = q * (HEAD_DIM ** -0.5)
    v = heads(v)
    ig = dir_pairs(i_pre) + i_bias.astype(jnp.float32)
    logf = jax.nn.log_sigmoid(dir_pairs(f_pre) + f_bias.astype(jnp.float32))
    h = (mlstm_chunked(q, k, v, ig[:, :, 0], logf[:, :, 0])
         + flip(mlstm_chunked(flip(q), flip(k), flip(v), flip(ig[:, :, 1]), flip(logf[:, :, 1]))))
    h = rmsnorm(h, norm_w) * jax.nn.sigmoid(heads(o_pre))
    return h.reshape(h.shape[:2] + (MIX_WIDTH,))


def encoder_layer(x, lb, norm_mix_pre, norm_mix_post, norm_ffn_pre, norm_ffn_post, w_in,
                  dn_conv, dn_a_log, dn_dt_bias, dn_norm, hg_norm,
                  ml_conv, ml_i_bias, ml_f_bias, ml_norm, w_branch, w_out, w_ff1, w_ff2):
    bsz, slen, _ = x.shape
    h = rmsnorm(x, norm_mix_pre)
    zin = jnp.matmul(h, w_in).astype(jnp.float32)
    (dq, dk, dv, dz, dbeta, dalpha, hq, hf, hi, hz,
     mq, mk, mv, mo, mi, mf, gpre) = split_cols(zin, IN_LAYOUT)
    o_dn = deltanet_branch(dq, dk, dv, dz, dbeta, dalpha, dn_conv, dn_a_log, dn_dt_bias, dn_norm)
    o_hg = hgrn2_branch(hq, hf, hi, hz, lb, hg_norm)
    o_ml = mlstm_branch(mq, mk, mv, mo, mi, mf, ml_conv, ml_i_bias, ml_f_bias, ml_norm)
    branches = jnp.stack([o_dn, o_hg, o_ml], axis=2).astype(x.dtype)
    gates = jax.nn.sigmoid(gpre).reshape(bsz, slen, N_BRANCH, D_MODEL).astype(x.dtype)
    merged = jnp.sum(gates * jnp.einsum('bsnc,ncd->bsnd', branches, w_branch), axis=2)
    x = x + rmsnorm(jnp.matmul(merged, w_out), norm_mix_post)
    u = jax.nn.relu(jnp.matmul(rmsnorm(x, norm_ffn_pre), w_ff1))
    x = x + rmsnorm(jnp.matmul(u * u, w_ff2), norm_ffn_post)
    return x


def setup_inputs(seed: int = 0) -> dict:
    key = jax.random.key(seed)
    ks = jax.random.split(key, 24)
    f32 = jnp.float32

    def nrm(k, shape, scale):
        return jax.random.normal(k, shape, f32) * scale

    def gain(k, shape):
        return 1.0 + 0.05 * jax.random.normal(k, shape, f32)

    dt = jnp.exp(jax.random.uniform(ks[9], (DEPTH, 2, N_HEADS), f32, math.log(1e-3), math.log(1e-1)))
    return {
        'x_prompt': jax.random.normal(ks[0], (BATCH, SEQ, D_MODEL), f32),
        'x_sample': jax.random.normal(ks[1], (DEC_BATCH, DEC_SEQ, D_MODEL), f32),
        'norm_mix_pre': gain(ks[2], (DEPTH, D_MODEL)),
        'norm_mix_post': gain(ks[3], (DEPTH, D_MODEL)),
        'norm_ffn_pre': gain(ks[4], (DEPTH, D_MODEL)),
        'norm_ffn_post': gain(ks[5], (DEPTH, D_MODEL)),
        'w_in': nrm(ks[6], (DEPTH, D_MODEL, IN_COLS), D_MODEL ** -0.5),
        'dn_conv': nrm(ks[7], (DEPTH, CONV_K, 3 * MIX_WIDTH), CONV_K ** -0.5),
        'dn_a_log': jnp.log(jax.random.uniform(ks[8], (DEPTH, 2, N_HEADS), f32, 1.0, 16.0)),
        'dn_dt_bias': dt + jnp.log(-jnp.expm1(-dt)),
        'dn_norm': gain(ks[10], (DEPTH, N_HEADS, HEAD_DIM)),
        'hg_lb': nrm(ks[11], (DEPTH, 2, MIX_WIDTH), 0.5),
        'hg_norm': gain(ks[12], (DEPTH, N_HEADS, HEAD_DIM)),
        'ml_conv': nrm(ks[13], (DEPTH, CONV_K, 2 * MIX_WIDTH), CONV_K ** -0.5),
        'ml_i_bias': nrm(ks[14], (DEPTH, 2, N_HEADS), 0.1),
        'ml_f_bias': jax.random.uniform(ks[15], (DEPTH, 2, N_HEADS), f32, 3.0, 6.0),
        'ml_norm': gain(ks[16], (DEPTH, N_HEADS, HEAD_DIM)),
        'w_branch': nrm(ks[17], (DEPTH, N_BRANCH, MIX_WIDTH, D_MODEL), MIX_WIDTH ** -0.5),
        'w_out': nrm(ks[18], (DEPTH, D_MODEL, D_MODEL), D_MODEL ** -0.5),
        'w_ff1': nrm(ks[19], (DEPTH, D_MODEL, D_FF), D_MODEL ** -0.5),
        'w_ff2': nrm(ks[20], (DEPTH, D_FF, D_MODEL), D_FF ** -0.5),
    }


def reference(x_prompt, x_sample, norm_mix_pre, norm_mix_post, norm_ffn_pre, norm_ffn_post, w_in,
              dn_conv, dn_a_log, dn_dt_bias, dn_norm, hg_lb, hg_norm,
              ml_conv, ml_i_bias, ml_f_bias, ml_norm, w_branch, w_out, w_ff1, w_ff2):
    p_layers = jax.nn.softmax(hg_lb.astype(jnp.float32), axis=0)
    lower_bounds = jnp.cumsum(p_layers, axis=0) - p_layers[0]
    y_prompt, y_sample = x_prompt, x_sample
    for l in range(DEPTH):
        layer = functools.partial(
            encoder_layer, lb=lower_bounds[l],
            norm_mix_pre=norm_mix_pre[l], norm_mix_post=norm_mix_post[l],
            norm_ffn_pre=norm_ffn_pre[l], norm_ffn_post=norm_ffn_post[l], w_in=w_in[l],
            dn_conv=dn_conv[l], dn_a_log=dn_a_log[l], dn_dt_bias=dn_dt_bias[l], dn_norm=dn_norm[l],
            hg_norm=hg_norm[l], ml_conv=ml_conv[l], ml_i_bias=ml_i_bias[l], ml_f_bias=ml_f_bias[l],
            ml_norm=ml_norm[l], w_branch=w_branch[l], w_out=w_out[l], w_ff1=w_ff1[l], w_ff2=w_ff2[l])
        y_prompt = layer(y_prompt)
        y_sample = layer(y_sample)
    return (y_prompt, y_sample)
```

```python
import functools

import jax
import jax.numpy as jnp
from jax import lax
from jax.experimental import pallas as pl
from jax.experimental.pallas import tpu as pltpu

D_MODEL = 1024
N_HEADS = 4
HEAD_DIM = 128
MIX_WIDTH = N_HEADS * HEAD_DIM
N_BRANCH = 3
CONV_K = 5
CONV_HALO = 8
D_FF = 4 * D_MODEL
EPS = 1e-6
NEG_BIG = -1e30
CHUNK = 128
SUB = 8
SOLVE_BASE = 16

GP_OFF = 0
DN_QKV_OFF = 3072
DN_Z_OFF = 4608
ML_QK_OFF = 5120
ML_V_OFF = 6144
ML_O_OFF = 6656
HG_Q_OFF = 7168
HG_F_OFF = 7680
HG_I_OFF = 8704
HG_Z_OFF = 9216
SM_OFF = 9728
N_COLS = 9856
SM_BETA, SM_ALPHA, SM_IG, SM_FG = 0, 8, 16, 24

PROJ_TM = 1024
PROJ_TN = 1408
MERGE_TM = 256
FFN_TM = 512
VMEM_LIMIT = 56 * 1024 * 1024

_F32 = jnp.float32
_BF16 = jnp.bfloat16


def _bdot(a, b):
    return jnp.dot(a.astype(_BF16), b.astype(_BF16), preferred_element_type=_F32)


def _bdot_nt(a, b):
    return lax.dot_general(a.astype(_BF16), b.astype(_BF16), (((1,), (1,)), ((), ())),
                           preferred_element_type=_F32)


def _bdot_tn(a, b):
    return lax.dot_general(a.astype(_BF16), b.astype(_BF16), (((0,), (0,)), ((), ())),
                           preferred_element_type=_F32)


def _hdot(a, b):
    return jnp.dot(a, b, preferred_element_type=_F32, precision=lax.Precision.HIGHEST)


def _sigmoid(x):
    return 1.0 / (1.0 + jnp.exp(-x))


def _silu(x):
    return x * _sigmoid(x)


def _softplus(x):
    return jnp.maximum(x, 0.0) + jnp.log(1.0 + jnp.exp(-jnp.abs(x)))


def _rmsnorm(x, g):
    return x * lax.rsqrt(jnp.mean(x * x, axis=-1, keepdims=True) + EPS) * g


def _tri(c, rev):
    row = lax.broadcasted_iota(jnp.int32, (c, c), 0)
    col = lax.broadcasted_iota(jnp.int32, (c, c), 1)
    return jnp.where((col >= row) if rev else (col <= row), 1.0, 0.0).astype(_F32)


def _conv_act(prev8, cur, next8, w, first, last):
    c = cur.shape[0]
    prev8 = jnp.where(first, 0.0, prev8)
    next8 = jnp.where(last, 0.0, next8)
    ext = jnp.concatenate([prev8, cur, next8], axis=0)
    base = CONV_HALO - CONV_K // 2
    y = w[0:1] * ext[base:base + c]
    for j in range(1, CONV_K):
        y = y + w[j:j + 1] * ext[base + j:base + j + c]
    return _silu(y)


def _head(t, h):
    return t[:, h * HEAD_DIM:(h + 1) * HEAD_DIM]


def _proj_kernel(x_ref, g_ref, w_ref, o_ref, h_scr):
    @pl.when(pl.program_id(1) == 0)
    def _():
        h_scr[...] = _rmsnorm(x_ref[...], g_ref[...]).astype(_BF16)

    o_ref[...] = jnp.dot(h_scr[...], w_ref[...], preferred_element_type=_F32)


def _proj(x2, g, w):
    t = x2.shape[0]
    tm = min(PROJ_TM, t)
    return pl.pallas_call(
        _proj_kernel,
        grid=(t // tm, N_COLS // PROJ_TN),
        in_specs=[pl.BlockSpec((tm, D_MODEL), lambda i, j: (i, 0)),
                  pl.BlockSpec((1, D_MODEL), lambda i, j: (0, 0)),
                  pl.BlockSpec((D_MODEL, PROJ_TN), lambda i, j: (0, j))],
        out_specs=pl.BlockSpec((tm, PROJ_TN), lambda i, j: (i, j)),
        out_shape=jax.ShapeDtypeStruct((t, N_COLS), _F32),
        scratch_shapes=[pltpu.VMEM((tm, D_MODEL), _BF16)],
        compiler_params=pltpu.CompilerParams(dimension_semantics=("arbitrary", "arbitrary"),
                                             vmem_limit_bytes=VMEM_LIMIT),
        name="proj",
    )(x2, g, w)


def _scan_specs(c, n_chunks, seq, width, off, halo):
    blk = off // width
    r8 = c // CONV_HALO
    last8 = seq // CONV_HALO - 1
    specs = []
    for rev in (False, True):
        def chunk(n, rev=rev):
            return (n_chunks - 1 - n) if rev else n
        specs.append(pl.BlockSpec((1, c, width), lambda b, n, f=chunk: (b, f(n), blk)))
        if halo:
            specs.append(pl.BlockSpec(
                (1, CONV_HALO, width), lambda b, n, f=chunk: (b, jnp.maximum(f(n) * r8 - 1, 0), blk)))
            specs.append(pl.BlockSpec(
                (1, CONV_HALO, width), lambda b, n, f=chunk: (b, jnp.minimum((f(n) + 1) * r8, last8), blk)))
    return specs


def _const_spec(shape):
    nd = len(shape)
    return pl.BlockSpec(shape, lambda b, n: (0,) * nd)


def _out_specs(c, n_chunks):
    return [pl.BlockSpec((1, c, MIX_WIDTH), lambda b, n: (b, n, 0)),
            pl.BlockSpec((1, c, MIX_WIDTH), lambda b, n: (b, n_chunks - 1 - n, 0))]


def _edge_flags(rev):
    n = pl.program_id(1)
    last = pl.num_programs(1) - 1
    return ((n == last), (n == 0)) if rev else ((n == 0), (n == last))


def _incl_mask(c, rev):
    row = lax.broadcasted_iota(jnp.int32, (c, c), 0)
    col = lax.broadcasted_iota(jnp.int32, (c, c), 1)
    return (col >= row) if rev else (col <= row)


def _strict_mask(c, rev):
    row = lax.broadcasted_iota(jnp.int32, (c, c), 0)
    col = lax.broadcasted_iota(jnp.int32, (c, c), 1)
    return (col > row) if rev else (col < row)


def _unit_tri_inverse(a):
    c = a.shape[0]
    row = lax.broadcasted_iota(jnp.int32, (c, c), 0)
    col = lax.broadcasted_iota(jnp.int32, (c, c), 1)
    blk = min(SOLVE_BASE, c)
    a0 = jnp.where((row // blk) == (col // blk), a, 0.0)
    t = jnp.where(row == col, 1.0, 0.0).astype(_F32) - a0
    pw = _bdot(a0, a0)
    k = 2
    while 2 * k < blk:
        both = _bdot(jnp.concatenate([t, pw], axis=0), pw)
        t = t + both[:c]
        pw = both[c:]
        k *= 2
    t = t + _bdot(t, pw)
    while blk < c:
        off = ((row // (2 * blk)) == (col // (2 * blk))) & ((row // blk) != (col // blk))
        t = t - _bdot(_bdot(t, jnp.where(off, a, 0.0)), t)
        blk *= 2
    return t


def _dn_kernel(qf, pf, nf, sf, qb, pb, nb, sb, cw_ref, gp_ref, of_ref, ob_ref, s_ref):
    c = qf.shape[1]

    @pl.when(pl.program_id(1) == 0)
    def _():
        s_ref[...] = jnp.zeros_like(s_ref)

    dt_bias = gp_ref[0:1, :]
    neg_a = -jnp.exp(gp_ref[1:2, :])
    for d, (q_ref, p_ref, n_ref, sm_ref, o_ref) in enumerate(
            ((qf, pf, nf, sf, of_ref), (qb, pb, nb, sb, ob_ref))):
        rev = d == 1
        first, last = _edge_flags(rev)
        x = _conv_act(p_ref[0], q_ref[0], n_ref[0], cw_ref[...], first, last)
        sm = sm_ref[0]
        beta_t = _sigmoid(sm)
        g_t = neg_a * _softplus(sm + dt_bias)
        gcum = _hdot(_tri(c, rev), g_t)
        gtot = jnp.sum(g_t, axis=0, keepdims=True)
        e_g = jnp.exp(gcum)
        e_rest = jnp.exp(gtot - gcum)
        e_tot = jnp.exp(gtot)
        gcum_t = gcum.T
        incl = _incl_mask(c, rev)
        strict = _strict_mask(c, rev)
        outs = []
        for h in range(N_HEADS):
            col = d * N_HEADS + h
            q = _head(x, h)
            k = _head(x, N_HEADS + h)
            v = _head(x, 2 * N_HEADS + h)
            q = q * lax.rsqrt(jnp.sum(q * q, axis=-1, keepdims=True) + EPS) * (HEAD_DIM ** -0.5)
            k = k * lax.rsqrt(jnp.sum(k * k, axis=-1, keepdims=True) + EPS)
            beta = beta_t[:, SM_BETA + col:SM_BETA + col + 1]
            g_col = gcum[:, SM_ALPHA + col:SM_ALPHA + col + 1]
            g_row = gcum_t[SM_ALPHA + col:SM_ALPHA + col + 1, :]
            eg = e_g[:, SM_ALPHA + col:SM_ALPHA + col + 1]
            er = e_rest[:, SM_ALPHA + col:SM_ALPHA + col + 1]
            et = e_tot[:, SM_ALPHA + col:SM_ALPHA + col + 1]
            decay = jnp.exp(jnp.where(incl, g_col - g_row, NEG_BIG))
            kb = k * beta
            both = _bdot_nt(jnp.concatenate([kb, q], axis=0), k)
            a_low = jnp.where(strict, both[:c] * decay, 0.0)
            attn = both[c:] * decay
            tinv = _unit_tri_inverse(a_low)
            sol = _bdot(tinv, jnp.concatenate([v * beta, kb * eg], axis=1))
            u = sol[:, :HEAD_DIM]
            w = sol[:, HEAD_DIM:]
            state = s_ref[d, h]
            ws_qs = _bdot(jnp.concatenate([w, q * eg], axis=0), state)
            v_new = u - ws_qs[:c]
            outs.append(ws_qs[c:] + _bdot(attn, v_new))
            s_ref[d, h] = state * et + _bdot_tn(k * er, v_new)
        o_ref[0] = jnp.concatenate(outs, axis=1)


def _dn_call(zin3, conv_w, gparams):
    b, s, _ = zin3.shape
    c = min(CHUNK, s)
    n_chunks = s // c
    f = _scan_specs(c, n_chunks, s, 3 * MIX_WIDTH, DN_QKV_OFF, True)
    sm = _scan_specs(c, n_chunks, s, HEAD_DIM, SM_OFF, False)
    in_specs = f[0:3] + [sm[0]] + f[3:6] + [sm[1]] + [_const_spec(conv_w.shape), _const_spec(gparams.shape)]
    out = jax.ShapeDtypeStruct((b, s, MIX_WIDTH), _F32)
    return pl.pallas_call(
        _dn_kernel,
        grid=(b, n_chunks),
        in_specs=in_specs,
        out_specs=_out_specs(c, n_chunks),
        out_shape=[out, out],
        scratch_shapes=[pltpu.VMEM((2, N_HEADS, HEAD_DIM, HEAD_DIM), _F32)],
        compiler_params=pltpu.CompilerParams(dimension_semantics=("arbitrary", "arbitrary"),
                                             vmem_limit_bytes=VMEM_LIMIT),
        name="deltanet",
    )(zin3, zin3, zin3, zin3, zin3, zin3, zin3, zin3, conv_w, gparams)


def _hg_level_masks(c, rev):
    row = lax.broadcasted_iota(jnp.int32, (c, c), 0)
    col = lax.broadcasted_iota(jnp.int32, (c, c), 1)
    r1 = lax.broadcasted_iota(jnp.int32, (c, 1), 0)
    levels = []
    blk = c // 2
    while blk >= SUB:
        same_pair = (row // (2 * blk)) == (col // (2 * blk))
        row_hi = (row // blk) % 2 == 1
        col_hi = (col // blk) % 2 == 1
        if rev:
            mask = same_pair & (~row_hi) & col_hi
            q_rows = (r1 // blk) % 2 == 0
        else:
            mask = same_pair & row_hi & (~col_hi)
            q_rows = (r1 // blk) % 2 == 1
        levels.append((blk, mask, q_rows))
        blk //= 2
    return levels


def _hg_band_masks(c, rev):
    row = lax.broadcasted_iota(jnp.int32, (c, c), 0)
    col = lax.broadcasted_iota(jnp.int32, (c, c), 1)
    masks = []
    for dl in range(SUB):
        if rev:
            masks.append((col == row + dl) & ((row % SUB) + dl < SUB))
        else:
            masks.append((col == row - dl) & ((row % SUB) >= dl))
    return masks


def _pair_reference(b, blk, rev):
    c, w = b.shape
    b3 = b.reshape(c // (2 * blk), 2 * blk, w)
    ref = b3[:, blk:blk + 1, :] if rev else b3[:, blk - 1:blk, :]
    return jnp.broadcast_to(ref, b3.shape).reshape(c, w)


def _hg_kernel(qf, ff, vf, qb, fb, vb, lb_ref, of_ref, ob_ref, s_ref):
    c = qf.shape[1]

    @pl.when(pl.program_id(1) == 0)
    def _():
        s_ref[...] = jnp.zeros_like(s_ref)

    for d, (q_ref, f_ref, v_ref, o_ref) in enumerate(((qf, ff, vf, of_ref), (qb, fb, vb, ob_ref))):
        rev = d == 1
        lb = lb_ref[d:d + 1, :]
        f_pre = f_ref[0]
        q_all = _silu(q_ref[0])
        k_all = (1.0 - lb) * _sigmoid(-f_pre)
        logf = jnp.log(lb + (1.0 - lb) * _sigmoid(f_pre))
        b_all = _hdot(_tri(c, rev), logf)
        btot = jnp.sum(logf, axis=0, keepdims=True)
        qe_all = q_all * jnp.exp(b_all)
        ke_all = k_all * jnp.exp(btot - b_all)
        e_tot = jnp.exp(btot)
        v_all = v_ref[0]
        levels = _hg_level_masks(c, rev)
        bands = _hg_band_masks(c, rev)
        outs = []
        for h in range(N_HEADS):
            q = _head(q_all, h)
            k = _head(k_all, h)
            v = _head(v_all, h)
            b = _head(b_all, h)
            attn = jnp.zeros((c, c), _F32)
            for blk, mask, q_rows in levels:
                ref = _pair_reference(b, blk, rev)
                e = jnp.exp(jnp.where(q_rows, b - ref, ref - b))
                qs = jnp.where(q_rows, q * e, 0.0)
                ks = jnp.where(q_rows, 0.0, k * e)
                attn = attn + jnp.where(mask, _bdot_nt(qs, ks), 0.0)
            for dl in range(SUB):
                if dl == 0:
                    s_dl = jnp.sum(q * k, axis=-1, keepdims=True)
                else:
                    sh = (c - dl) if rev else dl
                    k_dl = pltpu.roll(k, sh, 0)
                    b_dl = pltpu.roll(b, sh, 0)
                    s_dl = jnp.sum(q * k_dl * jnp.exp(b - b_dl), axis=-1, keepdims=True)
                attn = jnp.where(bands[dl], s_dl, attn)
            state_t = s_ref[d, h]
            outs.append(_bdot_nt(_head(qe_all, h), state_t) + _bdot(attn, v))
            s_ref[d, h] = state_t * _head(e_tot, h) + _bdot_tn(v, _head(ke_all, h))
        o_ref[0] = jnp.concatenate(outs, axis=1)


def _hg_call(zin3, lb):
    b, s, _ = zin3.shape
    c = min(CHUNK, s)
    n_chunks = s // c
    q = _scan_specs(c, n_chunks, s, MIX_WIDTH, HG_Q_OFF, False)
    f_f = _scan_specs(c, n_chunks, s, MIX_WIDTH, HG_F_OFF, False)[0]
    f_b = _scan_specs(c, n_chunks, s, MIX_WIDTH, HG_F_OFF + MIX_WIDTH, False)[1]
    v = _scan_specs(c, n_chunks, s, MIX_WIDTH, HG_I_OFF, False)
    in_specs = [q[0], f_f, v[0], q[1], f_b, v[1], _const_spec(lb.shape)]
    out = jax.ShapeDtypeStruct((b, s, MIX_WIDTH), _F32)
    return pl.pallas_call(
        _hg_kernel,
        grid=(b, n_chunks),
        in_specs=in_specs,
        out_specs=_out_specs(c, n_chunks),
        out_shape=[out, out],
        scratch_shapes=[pltpu.VMEM((2, N_HEADS, HEAD_DIM, HEAD_DIM), _F32)],
        compiler_params=pltpu.CompilerParams(dimension_semantics=("arbitrary", "arbitrary"),
                                             vmem_limit_bytes=VMEM_LIMIT),
        name="hgrn2",
    )(zin3, zin3, zin3, zin3, zin3, zin3, lb)


def _ml_kernel(qf, pf, nf, vf, sf, qb, pb, nb, vb, sb, cw_ref, gp_ref, of_ref, ob_ref, c_ref, m_ref):
    c = qf.shape[1]

    @pl.when(pl.program_id(1) == 0)
    def _():
        c_ref[...] = jnp.zeros_like(c_ref)
        m_ref[...] = jnp.full(m_ref.shape, NEG_BIG, _F32)

    lane = lax.broadcasted_iota(jnp.int32, (c, HEAD_DIM), 1)
    ones_col = jnp.where(lane == 0, 1.0, 0.0).astype(_F32)
    for d, (q_ref, p_ref, n_ref, v_ref, sm_ref, o_ref) in enumerate(
            ((qf, pf, nf, vf, sf, of_ref), (qb, pb, nb, vb, sb, ob_ref))):
        rev = d == 1
        first, last = _edge_flags(rev)
        x = _conv_act(p_ref[0], q_ref[0], n_ref[0], cw_ref[...], first, last)
        v_all = v_ref[0]
        sm = sm_ref[0]
        ig_t = sm + gp_ref[0:1, :]
        f_t = sm + gp_ref[1:2, :]
        logf = jnp.minimum(f_t, 0.0) - jnp.log(1.0 + jnp.exp(-jnp.abs(f_t)))
        bcum = _hdot(_tri(c, rev), logf)
        btot = jnp.sum(logf, axis=0, keepdims=True)
        comb_t = jnp.where((lane >= SM_FG) & (lane < SM_FG + 2 * N_HEADS), bcum, ig_t).T
        incl = _incl_mask(c, rev)
        outs = []
        for h in range(N_HEADS):
            col = d * N_HEADS + h
            q = _head(x, h) * (HEAD_DIM ** -0.5)
            k = _head(x, N_HEADS + h)
            v_ext = jnp.concatenate([_head(v_all, h), ones_col], axis=1)
            b_col = bcum[:, SM_FG + col:SM_FG + col + 1]
            i_col = ig_t[:, SM_IG + col:SM_IG + col + 1]
            b_row = comb_t[SM_FG + col:SM_FG + col + 1, :]
            i_row = comb_t[SM_IG + col:SM_IG + col + 1, :]
            bt = btot[:, SM_FG + col:SM_FG + col + 1]
            m_old = m_ref[col:col + 1, 0:1]
            logw = jnp.where(incl, b_col - b_row + i_row, NEG_BIG)
            inter = b_col + m_old
            m_row = jnp.maximum(inter, jnp.max(logw, axis=-1, keepdims=True))
            wmat = jnp.exp(logw - m_row)
            s_inter = jnp.exp(inter - m_row)
            qk = _bdot_nt(q, k) * wmat
            state = c_ref[d, h]
            nd = s_inter * _bdot(q, state) + _bdot(qk, v_ext)
            den = nd[:, HEAD_DIM:HEAD_DIM + 1]
            outs.append(nd[:, :HEAD_DIM] / jnp.maximum(jnp.abs(den), jnp.exp(-m_row)))
            logw_end = bt - b_col + i_col
            m_new = jnp.maximum(bt + m_old, jnp.max(logw_end, axis=0, keepdims=True))
            w_end = jnp.exp(logw_end - m_new)
            s_old = jnp.exp(bt + m_old - m_new)
            c_ref[d, h] = s_old * state + _bdot_tn(k * w_end, v_ext)
            m_ref[col:col + 1, :] = jnp.broadcast_to(m_new, (1, HEAD_DIM))
        o_ref[0] = jnp.concatenate(outs, axis=1)


def _ml_call(zin3, conv_w, gparams):
    b, s, _ = zin3.shape
    c = min(CHUNK, s)
    n_chunks = s // c
    f = _scan_specs(c, n_chunks, s, 2 * MIX_WIDTH, ML_QK_OFF, True)
    v = _scan_specs(c, n_chunks, s, MIX_WIDTH, ML_V_OFF, False)
    sm = _scan_specs(c, n_chunks, s, HEAD_DIM, SM_OFF, False)
    in_specs = (f[0:3] + [v[0], sm[0]] + f[3:6] + [v[1], sm[1]]
                + [_const_spec(conv_w.shape), _const_spec(gparams.shape)])
    out = jax.ShapeDtypeStruct((b, s, MIX_WIDTH), _F32)
    return pl.pallas_call(
        _ml_kernel,
        grid=(b, n_chunks),
        in_specs=in_specs,
        out_specs=_out_specs(c, n_chunks),
        out_shape=[out, out],
        scratch_shapes=[pltpu.VMEM((2, N_HEADS, HEAD_DIM, 2 * HEAD_DIM), _F32),
                        pltpu.VMEM((2 * N_HEADS, HEAD_DIM), _F32)],
        compiler_params=pltpu.CompilerParams(dimension_semantics=("arbitrary", "arbitrary"),
                                             vmem_limit_bytes=VMEM_LIMIT),
        name="mlstm",
    )(*([zin3] * 10), conv_w, gparams)


def _head_norm(o, w):
    parts = []
    for h in range(N_HEADS):
        seg = _head(o, h)
        parts.append(seg * lax.rsqrt(jnp.mean(seg * seg, axis=-1, keepdims=True) + EPS))
    return jnp.concatenate(parts, axis=1) * w


def _merge_kernel(x_ref, dnf, dnb, hgf, hgb, mlf, mlb, dnz, hgz, mlo, gp_ref,
                  nw_ref, wbr_ref, wout_ref, gpost_ref, o_ref):
    branches = (
        _head_norm(dnf[...] + dnb[...], nw_ref[0:1, :]) * _silu(dnz[...]),
        _head_norm(hgf[...] + hgb[...], nw_ref[1:2, :]) * _silu(hgz[...]),
        _head_norm(mlf[...] + mlb[...], nw_ref[2:3, :]) * _sigmoid(mlo[...]),
    )
    merged = None
    for i, br in enumerate(branches):
        gate = _sigmoid(gp_ref[:, i * D_MODEL:(i + 1) * D_MODEL])
        term = gate * _bdot(br, wbr_ref[i])
        merged = term if merged is None else merged + term
    y = _bdot(merged, wout_ref[...])
    o_ref[...] = x_ref[...] + _rmsnorm(y, gpost_ref[...])


def _merge_call(x2, zin2, mix_outs, norm_w, w_branch, w_out, g_post):
    t = x2.shape[0]
    tm = min(MERGE_TM, t)
    row = lambda w: pl.BlockSpec((tm, w), lambda i: (i, 0))
    zblk = lambda w, off: pl.BlockSpec((tm, w), lambda i: (i, off // w))
    full = lambda a: pl.BlockSpec(a.shape, lambda i: (0,) * a.ndim)
    in_specs = ([row(D_MODEL)] + [row(MIX_WIDTH)] * 6
                + [zblk(MIX_WIDTH, DN_Z_OFF), zblk(MIX_WIDTH, HG_Z_OFF), zblk(MIX_WIDTH, ML_O_OFF),
                   zblk(N_BRANCH * D_MODEL, GP_OFF)]
                + [full(norm_w), full(w_branch), full(w_out), full(g_post)])
    return pl.pallas_call(
        _merge_kernel,
        grid=(t // tm,),
        in_specs=in_specs,
        out_specs=row(D_MODEL),
        out_shape=jax.ShapeDtypeStruct((t, D_MODEL), _F32),
        compiler_params=pltpu.CompilerParams(dimension_semantics=("arbitrary",),
                                             vmem_limit_bytes=VMEM_LIMIT),
        name="merge",
    )(x2, *mix_outs, zin2, zin2, zin2, zin2, norm_w, w_branch, w_out, g_post)


def _ffn_kernel(x_ref, gpre_ref, w1_ref, w2_ref, gpost_ref, o_ref):
    x = x_ref[...]
    u = jnp.maximum(_bdot(_rmsnorm(x, gpre_ref[...]), w1_ref[...]), 0.0)
    y = _bdot(u * u, w2_ref[...])
    o_ref[...] = x + _rmsnorm(y, gpost_ref[...])


def _ffn_call(x2, g_pre, w1, w2, g_post):
    t = x2.shape[0]
    tm = min(FFN_TM, t)
    row = pl.BlockSpec((tm, D_MODEL), lambda i: (i, 0))
    full = lambda a: pl.BlockSpec(a.shape, lambda i: (0,) * a.ndim, pipeline_mode=pl.Buffered(1))
    return pl.pallas_call(
        _ffn_kernel,
        grid=(t // tm,),
        in_specs=[row, full(g_pre), full(w1), full(w2), full(g_post)],
        out_specs=row,
        out_shape=jax.ShapeDtypeStruct((t, D_MODEL), _F32),
        compiler_params=pltpu.CompilerParams(dimension_semantics=("arbitrary",),
                                             vmem_limit_bytes=VMEM_LIMIT),
        name="ffn",
    )(x2, g_pre, w1, w2, g_post)


def _permute_w_in(w_in):
    m = MIX_WIDTH
    o_dn, o_sm_dn, o_hg, o_ml, o_sm_ml, o_gp = 0, 4 * m, 4 * m + 16, 9 * m + 16, 13 * m + 16, 13 * m + 32
    parts = [
        w_in[:, o_gp:o_gp + N_BRANCH * D_MODEL],
        w_in[:, o_dn:o_dn + 4 * m],
        w_in[:, o_ml:o_ml + 4 * m],
        w_in[:, o_hg:o_hg + 5 * m],
        w_in[:, o_sm_dn:o_sm_dn + 16],
        w_in[:, o_sm_ml:o_sm_ml + 16],
        jnp.zeros((w_in.shape[0], HEAD_DIM - 32), w_in.dtype),
    ]
    return jnp.concatenate(parts, axis=1).astype(_BF16)


def _lane_row(values, off):
    row = jnp.zeros((HEAD_DIM,), _F32)
    return row.at[off:off + 2 * N_HEADS].set(values.astype(_F32).reshape(-1))


def _layer(x3, p):
    b, s, _ = x3.shape
    x2 = x3.reshape(b * s, D_MODEL)
    zin2 = _proj(x2, p["norm_mix_pre"], p["w_in"])
    zin3 = zin2.reshape(b, s, N_COLS)
    dn_f, dn_b = _dn_call(zin3, p["dn_conv"], p["dn_gates"])
    hg_f, hg_b = _hg_call(zin3, p["hg_lb"])
    ml_f, ml_b = _ml_call(zin3, p["ml_conv"], p["ml_gates"])
    mix = [a.reshape(b * s, MIX_WIDTH) for a in (dn_f, dn_b, hg_f, hg_b, ml_f, ml_b)]
    x2 = _merge_call(x2, zin2, mix, p["mix_norm"], p["w_branch"], p["w_out"], p["norm_mix_post"])
    x2 = _ffn_call(x2, p["norm_ffn_pre"], p["w_ff1"], p["w_ff2"], p["norm_ffn_post"])
    return x2.reshape(b, s, D_MODEL)


def kernel(x_prompt, x_sample, norm_mix_pre, norm_mix_post, norm_ffn_pre, norm_ffn_post, w_in, dn_conv,
           dn_a_log, dn_dt_bias, dn_norm, hg_lb, hg_norm, ml_conv, ml_i_bias, ml_f_bias, ml_norm,
           w_branch, w_out, w_ff1, w_ff2):
    depth = w_in.shape[0]
    p_layers = jax.nn.softmax(hg_lb.astype(_F32), axis=0)
    lower_bounds = jnp.cumsum(p_layers, axis=0) - p_layers[0]
    zeros_row = jnp.zeros((HEAD_DIM,), _F32)
    y_prompt, y_sample = x_prompt, x_sample
    for l in range(depth):
        p = {
            "norm_mix_pre": norm_mix_pre[l].reshape(1, D_MODEL),
            "norm_mix_post": norm_mix_post[l].reshape(1, D_MODEL),
            "norm_ffn_pre": norm_ffn_pre[l].reshape(1, D_MODEL),
            "norm_ffn_post": norm_ffn_post[l].reshape(1, D_MODEL),
            "w_in": _permute_w_in(w_in[l]),
            "dn_conv": dn_conv[l],
            "dn_gates": jnp.stack([_lane_row(dn_dt_bias[l], SM_ALPHA), _lane_row(dn_a_log[l], SM_ALPHA)]
                                  + [zeros_row] * 6),
            "hg_lb": lower_bounds[l],
            "ml_conv": ml_conv[l],
            "ml_gates": jnp.stack([_lane_row(ml_i_bias[l], SM_IG), _lane_row(ml_f_bias[l], SM_FG)]
                                  + [zeros_row] * 6),
            "mix_norm": jnp.stack([dn_norm[l].reshape(-1), hg_norm[l].reshape(-1), ml_norm[l].reshape(-1)]
                                  + [jnp.zeros((MIX_WIDTH,), _F32)] * 5),
            "w_branch": w_branch[l].astype(_BF16),
            "w_out": w_out[l].astype(_BF16),
            "w_ff1": w_ff1[l].astype(_BF16),
            "w_ff2": w_ff2[l].astype(_BF16),
        }
        y_prompt = _layer(y_prompt, p)
        y_sample = _layer(y_sample, p)
    return (y_prompt, y_sample)
```

```python
import jax
import jax.numpy as jnp
from jax import lax
from jax.experimental import pallas as pl
from jax.experimental.pallas import tpu as pltpu

D_MODEL = 1024
N_HEADS = 4
HEAD_DIM = 128
MIX_WIDTH = N_HEADS * HEAD_DIM
N_BRANCH = 3
CONV_K = 5
CONV_HALO = 8
D_FF = 4 * D_MODEL
EPS = 1e-6
NEG_BIG = -1e30
CHUNK = 128
SUB = 8
SOLVE_BASE = 16

GP_OFF = 0
DN_QKV_OFF = 3072
DN_Z_OFF = 4608
ML_QK_OFF = 5120
ML_V_OFF = 6144
ML_O_OFF = 6656
HG_Q_OFF = 7168
HG_F_OFF = 7680
HG_I_OFF = 8704
HG_Z_OFF = 9216
SM_OFF = 9728
N_COLS = 10240
SM_BETA, SM_ALPHA, SM_IG, SM_FG = 0, 8, 16, 24
DN_ACT_W = 3 * MIX_WIDTH
ML_ACT_W = 2 * MIX_WIDTH

PROJ_TM = 1024
PROJ_TN = 2048
PREP_TM = 256
MERGE_TM = 256
FFN_TM = 512
VMEM_LIMIT = 56 * 1024 * 1024

_F32 = jnp.float32
_BF16 = jnp.bfloat16
_CHAINS = tuple((d, h) for d in range(2) for h in range(N_HEADS))


def _bdot(a, b):
    return jnp.dot(a.astype(_BF16), b.astype(_BF16), preferred_element_type=_F32)


def _bdot_nt(a, b):
    return lax.dot_general(a.astype(_BF16), b.astype(_BF16), (((1,), (1,)), ((), ())),
                           preferred_element_type=_F32)


def _bdot_tn(a, b):
    return lax.dot_general(a.astype(_BF16), b.astype(_BF16), (((0,), (0,)), ((), ())),
                           preferred_element_type=_F32)


def _sigmoid(x):
    return 0.5 * jnp.tanh(0.5 * x) + 0.5


def _silu(x):
    h = 0.5 * x
    return h * jnp.tanh(h) + h


def _softplus(x):
    return jnp.maximum(x, 0.0) + jnp.log(1.0 + jnp.exp(-jnp.abs(x)))


def _rmsnorm(x, g):
    return x * lax.rsqrt(jnp.mean(x * x, axis=-1, keepdims=True) + EPS) * g


def _cumsum_rows(x, rev):
    c, w = x.shape
    row = lax.broadcasted_iota(jnp.int32, (c, c), 0)
    col = lax.broadcasted_iota(jnp.int32, (c, c), 1)
    tri = jnp.where((col >= row) if rev else (col <= row), 1.0, 0.0).astype(_BF16)
    hi = x.astype(_BF16)
    r1 = x - hi.astype(_F32)
    mid = r1.astype(_BF16)
    lo = (r1 - mid.astype(_F32)).astype(_BF16)
    s = jnp.dot(tri, jnp.concatenate([hi, mid, lo], axis=1), preferred_element_type=_F32)
    return s[:, :w] + s[:, w:2 * w] + s[:, 2 * w:]


def _head(t, h):
    return t[:, h * HEAD_DIM:(h + 1) * HEAD_DIM]


def _proj_kernel(x_ref, g_ref, w_ref, o_ref, h_scr):
    @pl.when(pl.program_id(1) == 0)
    def _():
        h_scr[...] = _rmsnorm(x_ref[...], g_ref[...]).astype(_BF16)

    o_ref[...] = jnp.dot(h_scr[...], w_ref[...], preferred_element_type=_F32)


def _proj(x2, g, w):
    t = x2.shape[0]
    tm = min(PROJ_TM, t)
    return pl.pallas_call(
        _proj_kernel,
        grid=(t // tm, N_COLS // PROJ_TN),
        in_specs=[pl.BlockSpec((tm, D_MODEL), lambda i, j: (i, 0)),
                  pl.BlockSpec((1, D_MODEL), lambda i, j: (0, 0)),
                  pl.BlockSpec((D_MODEL, PROJ_TN), lambda i, j: (0, j))],
        out_specs=pl.BlockSpec((tm, PROJ_TN), lambda i, j: (i, j)),
        out_shape=jax.ShapeDtypeStruct((t, N_COLS), _F32),
        scratch_shapes=[pltpu.VMEM((tm, D_MODEL), _BF16)],
        compiler_params=pltpu.CompilerParams(dimension_semantics=("arbitrary", "arbitrary"),
                                             vmem_limit_bytes=VMEM_LIMIT),
        name="proj",
    )(x2, g, w)


def _conv_act(prev8, cur, next8, w, first, last):
    c = cur.shape[0]
    prev8 = jnp.where(first, 0.0, prev8)
    next8 = jnp.where(last, 0.0, next8)
    ext = jnp.concatenate([prev8, cur, next8], axis=0)
    rows = c + 2 * CONV_HALO
    y = None
    for j in range(CONV_K):
        sh = (CONV_K // 2 - j) % rows
        tap = ext if sh == 0 else pltpu.roll(ext, sh, 0)
        term = w[j:j + 1] * tap[CONV_HALO:CONV_HALO + c]
        y = term if y is None else y + term
    return _silu(y)


def _prep_kernel(dq, dp, dn, mq, mp, mn, dw_ref, mw_ref, dn_out, ml_out):
    n = pl.program_id(1)
    first = n == 0
    last = n == pl.num_programs(1) - 1
    x = _conv_act(dp[0], dq[0], dn[0], dw_ref[...], first, last)
    parts = []
    for h in range(2 * N_HEADS):
        t = _head(x, h)
        t = t * lax.rsqrt(jnp.sum(t * t, axis=-1, keepdims=True) + EPS)
        parts.append(t * (HEAD_DIM ** -0.5) if h < N_HEADS else t)
    parts.append(x[:, 2 * MIX_WIDTH:])
    dn_out[0] = jnp.concatenate(parts, axis=1).astype(_BF16)
    y = _conv_act(mp[0], mq[0], mn[0], mw_ref[...], first, last)
    ml_out[0] = jnp.concatenate([y[:, :MIX_WIDTH] * (HEAD_DIM ** -0.5), y[:, MIX_WIDTH:]], axis=1).astype(_BF16)


def _halo_specs(tm, seq, width, off):
    blk = off // width
    r8 = tm // CONV_HALO
    last8 = seq // CONV_HALO - 1
    return [pl.BlockSpec((1, tm, width), lambda b, n: (b, n, blk)),
            pl.BlockSpec((1, CONV_HALO, width), lambda b, n: (b, jnp.maximum(n * r8 - 1, 0), blk)),
            pl.BlockSpec((1, CONV_HALO, width), lambda b, n: (b, jnp.minimum((n + 1) * r8, last8), blk))]


def _prep_call(zin3, dn_conv, ml_conv):
    b, s, _ = zin3.shape
    tm = min(PREP_TM, s)
    in_specs = (_halo_specs(tm, s, DN_ACT_W, DN_QKV_OFF) + _halo_specs(tm, s, ML_ACT_W, ML_QK_OFF)
                + [_const_spec(dn_conv.shape), _const_spec(ml_conv.shape)])
    return pl.pallas_call(
        _prep_kernel,
        grid=(b, s // tm),
        in_specs=in_specs,
        out_specs=[pl.BlockSpec((1, tm, DN_ACT_W), lambda b, n: (b, n, 0)),
                   pl.BlockSpec((1, tm, ML_ACT_W), lambda b, n: (b, n, 0))],
        out_shape=[jax.ShapeDtypeStruct((b, s, DN_ACT_W), _BF16),
                   jax.ShapeDtypeStruct((b, s, ML_ACT_W), _BF16)],
        compiler_params=pltpu.CompilerParams(dimension_semantics=("arbitrary", "arbitrary"),
                                             vmem_limit_bytes=VMEM_LIMIT),
        name="prep",
    )(zin3, zin3, zin3, zin3, zin3, zin3, dn_conv, ml_conv)


def _pair_specs(c, n_chunks, width, off):
    blk = off // width
    return [pl.BlockSpec((1, c, width), lambda b, n: (b, n, blk)),
            pl.BlockSpec((1, c, width), lambda b, n: (b, n_chunks - 1 - n, blk))]


def _const_spec(shape):
    nd = len(shape)
    return pl.BlockSpec(shape, lambda b, n: (0,) * nd)


def _scan_call(kernel, name, b, s, in_specs, scratch_shapes, args):
    c = min(CHUNK, s)
    n_chunks = s // c
    out = jax.ShapeDtypeStruct((b, s, MIX_WIDTH), _BF16)
    return pl.pallas_call(
        kernel,
        grid=(b, n_chunks),
        in_specs=in_specs,
        out_specs=_pair_specs(c, n_chunks, MIX_WIDTH, 0),
        out_shape=[out, out],
        scratch_shapes=scratch_shapes,
        compiler_params=pltpu.CompilerParams(dimension_semantics=("arbitrary", "arbitrary"),
                                             vmem_limit_bytes=VMEM_LIMIT),
        name=name,
    )(*args)


def _tri_masks(c, rev):
    row = lax.broadcasted_iota(jnp.int32, (c, c), 0)
    col = lax.broadcasted_iota(jnp.int32, (c, c), 1)
    return ((col >= row), (col > row)) if rev else ((col <= row), (col < row))


def _unit_tri_inverse_many(mats):
    c = mats[0].shape[0]
    row = lax.broadcasted_iota(jnp.int32, (c, c), 0)
    col = lax.broadcasted_iota(jnp.int32, (c, c), 1)
    blk = min(SOLVE_BASE, c)
    eye = jnp.where(row == col, 1.0, 0.0).astype(_F32)
    diag_blk = (row // blk) == (col // blk)
    a0 = [jnp.where(diag_blk, a, 0.0) for a in mats]
    ts = [eye - x for x in a0]
    pws = [_bdot(x, x) for x in a0]
    k = 2
    while 2 * k < blk:
        boths = [_bdot(jnp.concatenate([t, pw], axis=0), pw) for t, pw in zip(ts, pws)]
        ts = [t + both[:c] for t, both in zip(ts, boths)]
        pws = [both[c:] for both in boths]
        k *= 2
    ts = [t + _bdot(t, pw) for t, pw in zip(ts, pws)]
    while blk < c:
        off = ((row // (2 * blk)) == (col // (2 * blk))) & ((row // blk) != (col // blk))
        tl = [_bdot(t, jnp.where(off, a, 0.0)) for t, a in zip(ts, mats)]
        ts = [t - _bdot(x, t) for t, x in zip(ts, tl)]
        blk *= 2
    return ts


def _dn_kernel(af, sf, ab, sb, gp_ref, of_ref, ob_ref, s_ref):
    c = af.shape[1]

    @pl.when(pl.program_id(1) == 0)
    def _():
        s_ref[...] = jnp.zeros_like(s_ref)

    dt_bias = gp_ref[0:1, :]
    neg_a = -jnp.exp(gp_ref[1:2, :])
    gates = []
    for rev, sm_ref in ((False, sf), (True, sb)):
        sm = sm_ref[0]
        g_t = neg_a * _softplus(sm + dt_bias)
        gcum = _cumsum_rows(g_t, rev)
        gtot = jnp.sum(g_t, axis=0, keepdims=True)
        gates.append(dict(beta=_sigmoid(sm), gcum=gcum, gcum_t=gcum.T, e_g=jnp.exp(gcum),
                          e_rest=jnp.exp(gtot - gcum), e_tot=jnp.exp(gtot), masks=_tri_masks(c, rev)))
    acts = (af, ab)
    q, k, kb, vb, eg, a_low, attn = [], [], [], [], [], [], []
    for d, h in _CHAINS:
        g = gates[d]
        lane_a = SM_ALPHA + d * N_HEADS + h
        lane_b = SM_BETA + d * N_HEADS + h
        incl, strict = g["masks"]
        qh = acts[d][0, :, h * HEAD_DIM:(h + 1) * HEAD_DIM]
        kh = acts[d][0, :, MIX_WIDTH + h * HEAD_DIM:MIX_WIDTH + (h + 1) * HEAD_DIM]
        vh = acts[d][0, :, 2 * MIX_WIDTH + h * HEAD_DIM:2 * MIX_WIDTH + (h + 1) * HEAD_DIM]
        beta = g["beta"][:, lane_b:lane_b + 1]
        decay = jnp.exp(jnp.where(incl, g["gcum"][:, lane_a:lane_a + 1] - g["gcum_t"][lane_a:lane_a + 1, :],
                                  NEG_BIG))
        kbh = kh.astype(_F32) * beta
        both = _bdot_nt(jnp.concatenate([kbh.astype(_BF16), qh], axis=0), kh)
        a_low.append(jnp.where(strict, both[:c] * decay, 0.0))
        attn.append(both[c:] * decay)
        q.append(qh)
        k.append(kh)
        kb.append(kbh)
        vb.append(vh.astype(_F32) * beta)
        eg.append(g["e_g"][:, lane_a:lane_a + 1])
    tinv = _unit_tri_inverse_many(a_low)
    sol = [_bdot(t, jnp.concatenate([vb[i], kb[i] * eg[i]], axis=1)) for i, t in enumerate(tinv)]
    ws_qs = [_bdot(jnp.concatenate([sol[i][:, HEAD_DIM:], q[i].astype(_F32) * eg[i]], axis=0), s_ref[d, h])
             for i, (d, h) in enumerate(_CHAINS)]
    v_new = [sol[i][:, :HEAD_DIM] - ws_qs[i][:c] for i in range(len(_CHAINS))]
    outs = [ws_qs[i][c:] + _bdot(attn[i], v_new[i]) for i in range(len(_CHAINS))]
    for i, (d, h) in enumerate(_CHAINS):
        lane_a = SM_ALPHA + d * N_HEADS + h
        er = gates[d]["e_rest"][:, lane_a:lane_a + 1]
        et = gates[d]["e_tot"][:, lane_a:lane_a + 1]
        s_ref[d, h] = s_ref[d, h] * et + _bdot_tn(k[i].astype(_F32) * er, v_new[i])
    of_ref[0] = jnp.concatenate(outs[:N_HEADS], axis=1).astype(_BF16)
    ob_ref[0] = jnp.concatenate(outs[N_HEADS:], axis=1).astype(_BF16)


def _dn_call(dn_act, zin3, gparams):
    b, s, _ = zin3.shape
    c = min(CHUNK, s)
    n_chunks = s // c
    a = _pair_specs(c, n_chunks, DN_ACT_W, 0)
    sm = _pair_specs(c, n_chunks, HEAD_DIM, SM_OFF)
    in_specs = [a[0], sm[0], a[1], sm[1], _const_spec(gparams.shape)]
    scratch = [pltpu.VMEM((2, N_HEADS, HEAD_DIM, HEAD_DIM), _F32)]
    return _scan_call(_dn_kernel, "deltanet", b, s, in_specs, scratch, (dn_act, zin3, dn_act, zin3, gparams))


def _hg_level_masks(c, rev):
    row = lax.broadcasted_iota(jnp.int32, (c, c), 0)
    col = lax.broadcasted_iota(jnp.int32, (c, c), 1)
    r1 = lax.broadcasted_iota(jnp.int32, (c, 1), 0)
    levels = []
    blk = c // 2
    while blk >= SUB:
        same_pair = (row // (2 * blk)) == (col // (2 * blk))
        row_hi = (row // blk) % 2 == 1
        col_hi = (col // blk) % 2 == 1
        if rev:
            mask = same_pair & (~row_hi) & col_hi
            q_rows = (r1 // blk) % 2 == 0
        else:
            mask = same_pair & row_hi & (~col_hi)
            q_rows = (r1 // blk) % 2 == 1
        levels.append((blk, mask, q_rows))
        blk //= 2
    return levels


def _hg_band_masks(c, rev):
    row = lax.broadcasted_iota(jnp.int32, (c, c), 0)
    col = lax.broadcasted_iota(jnp.int32, (c, c), 1)
    masks = []
    for dl in range(SUB):
        if rev:
            masks.append((col == row + dl) & ((row % SUB) + dl < SUB))
        else:
            masks.append((col == row - dl) & ((row % SUB) >= dl))
    return masks


def _pair_reference(b, blk, rev):
    c, w = b.shape
    b3 = b.reshape(c // (2 * blk), 2 * blk, w)
    ref = b3[:, blk:blk + 1, :] if rev else b3[:, blk - 1:blk, :]
    return jnp.broadcast_to(ref, b3.shape).reshape(c, w)


def _hg_kernel(qf, ff, vf, qb, fb, vb, lb_ref, of_ref, ob_ref, s_ref):
    c = qf.shape[1]

    @pl.when(pl.program_id(1) == 0)
    def _():
        s_ref[...] = jnp.zeros_like(s_ref)

    for d, (q_ref, f_ref, v_ref, o_ref) in enumerate(((qf, ff, vf, of_ref), (qb, fb, vb, ob_ref))):
        rev = d == 1
        lb = lb_ref[d:d + 1, :]
        sig = _sigmoid(f_ref[0])
        q_all = _silu(q_ref[0])
        k_all = (1.0 - lb) * (1.0 - sig)
        logf = jnp.log(lb + (1.0 - lb) * sig)
        b_all = _cumsum_rows(logf, rev)
        btot = jnp.sum(logf, axis=0, keepdims=True)
        qe_all = q_all * jnp.exp(b_all)
        ke_all = k_all * jnp.exp(btot - b_all)
        e_tot = jnp.exp(btot)
        v_all = v_ref[0]
        levels = _hg_level_masks(c, rev)
        bands = _hg_band_masks(c, rev)
        outs = []
        for h in range(N_HEADS):
            q = _head(q_all, h)
            k = _head(k_all, h)
            v = _head(v_all, h)
            b = _head(b_all, h)
            attn = jnp.zeros((c, c), _F32)
            for blk, mask, q_rows in levels:
                ref = _pair_reference(b, blk, rev)
                e = jnp.exp(jnp.where(q_rows, b - ref, ref - b))
                qs = jnp.where(q_rows, q * e, 0.0)
                ks = jnp.where(q_rows, 0.0, k * e)
                attn = attn + jnp.where(mask, _bdot_nt(qs, ks), 0.0)
            for dl in range(SUB):
                if dl == 0:
                    s_dl = jnp.sum(q * k, axis=-1, keepdims=True)
                else:
                    sh = (c - dl) if rev else dl
                    k_dl = pltpu.roll(k, sh, 0)
                    b_dl = pltpu.roll(b, sh, 0)
                    s_dl = jnp.sum(q * k_dl * jnp.exp(b - b_dl), axis=-1, keepdims=True)
                attn = jnp.where(bands[dl], s_dl, attn)
            state_t = s_ref[d, h]
            outs.append(_bdot_nt(_head(qe_all, h), state_t) + _bdot(attn, v))
            s_ref[d, h] = state_t * _head(e_tot, h) + _bdot_tn(v, _head(ke_all, h))
        o_ref[0] = jnp.concatenate(outs, axis=1).astype(_BF16)


def _hg_call(zin3, lb):
    b, s, _ = zin3.shape
    c = min(CHUNK, s)
    n_chunks = s // c
    q = _pair_specs(c, n_chunks, MIX_WIDTH, HG_Q_OFF)
    f_f = _pair_specs(c, n_chunks, MIX_WIDTH, HG_F_OFF)[0]
    f_b = _pair_specs(c, n_chunks, MIX_WIDTH, HG_F_OFF + MIX_WIDTH)[1]
    v = _pair_specs(c, n_chunks, MIX_WIDTH, HG_I_OFF)
    in_specs = [q[0], f_f, v[0], q[1], f_b, v[1], _const_spec(lb.shape)]
    scratch = [pltpu.VMEM((2, N_HEADS, HEAD_DIM, HEAD_DIM), _F32)]
    return _scan_call(_hg_kernel, "hgrn2", b, s, in_specs, scratch, (zin3,) * 6 + (lb,))


def _ml_kernel(af, vf, sf, ab, vb, sb, gp_ref, of_ref, ob_ref, c_ref, m_ref):
    c = af.shape[1]

    @pl.when(pl.program_id(1) == 0)
    def _():
        c_ref[...] = jnp.zeros_like(c_ref)
        m_ref[...] = jnp.full(m_ref.shape, NEG_BIG, _F32)

    lane = lax.broadcasted_iota(jnp.int32, (c, HEAD_DIM), 1)
    ones_col = jnp.where(lane == 0, 1.0, 0.0).astype(_BF16)
    gates = []
    for rev, sm_ref in ((False, sf), (True, sb)):
        sm = sm_ref[0]
        ig_t = sm + gp_ref[0:1, :]
        f_t = sm + gp_ref[1:2, :]
        logf = jnp.minimum(f_t, 0.0) - jnp.log(1.0 + jnp.exp(-jnp.abs(f_t)))
        bcum = _cumsum_rows(logf, rev)
        comb_t = jnp.where((lane >= SM_FG) & (lane < SM_FG + 2 * N_HEADS), bcum, ig_t).T
        gates.append(dict(ig=ig_t, bcum=bcum, btot=jnp.sum(logf, axis=0, keepdims=True), comb_t=comb_t,
                          incl=_tri_masks(c, rev)[0]))
    acts = (af, ab)
    vals = (vf, vb)
    q, k, v_ext, qk, qs = [], [], [], [], []
    for d, h in _CHAINS:
        qh = acts[d][0, :, h * HEAD_DIM:(h + 1) * HEAD_DIM]
        kh = acts[d][0, :, MIX_WIDTH + h * HEAD_DIM:MIX_WIDTH + (h + 1) * HEAD_DIM]
        vh = vals[d][0, :, h * HEAD_DIM:(h + 1) * HEAD_DIM]
        q.append(qh)
        k.append(kh)
        v_ext.append(jnp.concatenate([vh.astype(_BF16), ones_col], axis=1))
        qk.append(_bdot_nt(qh, kh))
        qs.append(_bdot(qh, c_ref[d, h]))
    n_ch = len(_CHAINS)
    logw, inter, logw_end, bt_m = [], [], [], []
    for d, h in _CHAINS:
        g = gates[d]
        col = d * N_HEADS + h
        b_col = g["bcum"][:, SM_FG + col:SM_FG + col + 1]
        i_col = g["ig"][:, SM_IG + col:SM_IG + col + 1]
        b_row = g["comb_t"][SM_FG + col:SM_FG + col + 1, :]
        i_row = g["comb_t"][SM_IG + col:SM_IG + col + 1, :]
        bt = g["btot"][:, SM_FG + col:SM_FG + col + 1]
        m_old = m_ref[col:col + 1, 0:1]
        logw.append(jnp.where(g["incl"], b_col - b_row + i_row, NEG_BIG))
        inter.append(b_col + m_old)
        logw_end.append(bt - b_col + i_col)
        bt_m.append(bt + m_old)
    m_row = [jnp.maximum(inter[i], jnp.max(logw[i], axis=-1, keepdims=True)) for i in range(n_ch)]
    m_new = [jnp.maximum(bt_m[i], jnp.max(logw_end[i], axis=0, keepdims=True)) for i in range(n_ch)]
    p = [qk[i] * jnp.exp(logw[i] - m_row[i]) for i in range(n_ch)]
    pv = [_bdot(p[i], v_ext[i]) for i in range(n_ch)]
    outs = []
    for i in range(n_ch):
        nd = jnp.exp(inter[i] - m_row[i]) * qs[i] + pv[i]
        den = nd[:, HEAD_DIM:HEAD_DIM + 1]
        outs.append(nd[:, :HEAD_DIM] / jnp.maximum(jnp.abs(den), jnp.exp(-m_row[i])))
    kw = [k[i].astype(_F32) * jnp.exp(logw_end[i] - m_new[i]) for i in range(n_ch)]
    for i, (d, h) in enumerate(_CHAINS):
        col = d * N_HEADS + h
        c_ref[d, h] = jnp.exp(bt_m[i] - m_new[i]) * c_ref[d, h] + _bdot_tn(kw[i], v_ext[i])
        m_ref[col:col + 1, :] = jnp.broadcast_to(m_new[i], (1, HEAD_DIM))
    of_ref[0] = jnp.concatenate(outs[:N_HEADS], axis=1).astype(_BF16)
    ob_ref[0] = jnp.concatenate(outs[N_HEADS:], axis=1).astype(_BF16)


def _ml_call(ml_act, zin3, gparams):
    b, s, _ = zin3.shape
    c = min(CHUNK, s)
    n_chunks = s // c
    a = _pair_specs(c, n_chunks, ML_ACT_W, 0)
    v = _pair_specs(c, n_chunks, MIX_WIDTH, ML_V_OFF)
    sm = _pair_specs(c, n_chunks, HEAD_DIM, SM_OFF)
    in_specs = [a[0], v[0], sm[0], a[1], v[1], sm[1], _const_spec(gparams.shape)]
    scratch = [pltpu.VMEM((2, N_HEADS, HEAD_DIM, 2 * HEAD_DIM), _F32),
               pltpu.VMEM((2 * N_HEADS, HEAD_DIM), _F32)]
    return _scan_call(_ml_kernel, "mlstm", b, s, in_specs, scratch,
                      (ml_act, zin3, zin3, ml_act, zin3, zin3, gparams))


def _head_norm(o, w):
    parts = []
    for h in range(N_HEADS):
        seg = _head(o, h)
        parts.append(seg * lax.rsqrt(jnp.mean(seg * seg, axis=-1, keepdims=True) + EPS))
    return jnp.concatenate(parts, axis=1) * w


def _merge_kernel(x_ref, dnf, dnb, hgf, hgb, mlf, mlb, dnz, hgz, mlo, gp_ref,
                  nw_ref, wbr_ref, wout_ref, gpost_ref, o_ref):
    both = lambda f, b: f[...].astype(_F32) + b[...].astype(_F32)
    branches = (
        _head_norm(both(dnf, dnb), nw_ref[0:1, :]) * _silu(dnz[...]),
        _head_norm(both(hgf, hgb), nw_ref[1:2, :]) * _silu(hgz[...]),
        _head_norm(both(mlf, mlb), nw_ref[2:3, :]) * _sigmoid(mlo[...]),
    )
    merged = None
    for i, br in enumerate(branches):
        gate = _sigmoid(gp_ref[:, i * D_MODEL:(i + 1) * D_MODEL])
        term = gate * _bdot(br, wbr_ref[i])
        merged = term if merged is None else merged + term
    y = _bdot(merged, wout_ref[...])
    o_ref[...] = x_ref[...] + _rmsnorm(y, gpost_ref[...])


def _merge_call(x2, zin2, mix_outs, norm_w, w_branch, w_out, g_post):
    t = x2.shape[0]
    tm = min(MERGE_TM, t)
    row = lambda w: pl.BlockSpec((tm, w), lambda i: (i, 0))
    zblk = lambda w, off: pl.BlockSpec((tm, w), lambda i: (i, off // w))
    full = lambda a: pl.BlockSpec(a.shape, lambda i: (0,) * a.ndim)
    in_specs = ([row(D_MODEL)] + [row(MIX_WIDTH)] * 6
                + [zblk(MIX_WIDTH, DN_Z_OFF), zblk(MIX_WIDTH, HG_Z_OFF), zblk(MIX_WIDTH, ML_O_OFF),
                   zblk(N_BRANCH * D_MODEL, GP_OFF)]
                + [full(norm_w), full(w_branch), full(w_out), full(g_post)])
    return pl.pallas_call(
        _merge_kernel,
        grid=(t // tm,),
        in_specs=in_specs,
        out_specs=row(D_MODEL),
        out_shape=jax.ShapeDtypeStruct((t, D_MODEL), _F32),
        compiler_params=pltpu.CompilerParams(dimension_semantics=("arbitrary",),
                                             vmem_limit_bytes=VMEM_LIMIT),
        name="merge",
    )(x2, *mix_outs, zin2, zin2, zin2, zin2, norm_w, w_branch, w_out, g_post)


def _ffn_kernel(x_ref, gpre_ref, w1_ref, w2_ref, gpost_ref, o_ref):
    x = x_ref[...]
    u = jnp.maximum(_bdot(_rmsnorm(x, gpre_ref[...]), w1_ref[...]), 0.0)
    y = _bdot(u * u, w2_ref[...])
    o_ref[...] = x + _rmsnorm(y, gpost_ref[...])


def _ffn_call(x2, g_pre, w1, w2, g_post):
    t = x2.shape[0]
    tm = min(FFN_TM, t)
    row = pl.BlockSpec((tm, D_MODEL), lambda i: (i, 0))
    full = lambda a: pl.BlockSpec(a.shape, lambda i: (0,) * a.ndim, pipeline_mode=pl.Buffered(1))
    return pl.pallas_call(
        _ffn_kernel,
        grid=(t // tm,),
        in_specs=[row, full(g_pre), full(w1), full(w2), full(g_post)],
        out_specs=row,
        out_shape=jax.ShapeDtypeStruct((t, D_MODEL), _F32),
        compiler_params=pltpu.CompilerParams(dimension_semantics=("arbitrary",),
                                             vmem_limit_bytes=VMEM_LIMIT),
        name="ffn",
    )(x2, g_pre, w1, w2, g_post)


def _permute_w_in(w_in):
    m = MIX_WIDTH
    o_dn, o_sm_dn, o_hg, o_ml, o_sm_ml, o_gp = 0, 4 * m, 4 * m + 16, 9 * m + 16, 13 * m + 16, 13 * m + 32
    parts = [
        w_in[:, o_gp:o_gp + N_BRANCH * D_MODEL],
        w_in[:, o_dn:o_dn + 4 * m],
        w_in[:, o_ml:o_ml + 4 * m],
        w_in[:, o_hg:o_hg + 5 * m],
        w_in[:, o_sm_dn:o_sm_dn + 16],
        w_in[:, o_sm_ml:o_sm_ml + 16],
        jnp.zeros((w_in.shape[0], N_COLS - SM_OFF - 32), w_in.dtype),
    ]
    return jnp.concatenate(parts, axis=1).astype(_BF16)


def _lane_row(values, off):
    row = jnp.zeros((HEAD_DIM,), _F32)
    return row.at[off:off + 2 * N_HEADS].set(values.astype(_F32).reshape(-1))


def _layer(x3, p):
    b, s, _ = x3.shape
    x2 = x3.reshape(b * s, D_MODEL)
    zin2 = _proj(x2, p["norm_mix_pre"], p["w_in"])
    zin3 = zin2.reshape(b, s, N_COLS)
    dn_act, ml_act = _prep_call(zin3, p["dn_conv"], p["ml_conv"])
    dn_f, dn_b = _dn_call(dn_act, zin3, p["dn_gates"])
    hg_f, hg_b = _hg_call(zin3, p["hg_lb"])
    ml_f, ml_b = _ml_call(ml_act, zin3, p["ml_gates"])
    mix = [a.reshape(b * s, MIX_WIDTH) for a in (dn_f, dn_b, hg_f, hg_b, ml_f, ml_b)]
    x2 = _merge_call(x2, zin2, mix, p["mix_norm"], p["w_branch"], p["w_out"], p["norm_mix_post"])
    x2 = _ffn_call(x2, p["norm_ffn_pre"], p["w_ff1"], p["w_ff2"], p["norm_ffn_post"])
    return x2.reshape(b, s, D_MODEL)


def kernel(x_prompt, x_sample, norm_mix_pre, norm_mix_post, norm_ffn_pre, norm_ffn_post, w_in, dn_conv,
           dn_a_log, dn_dt_bias, dn_norm, hg_lb, hg_norm, ml_conv, ml_i_bias, ml_f_bias, ml_norm,
           w_branch, w_out, w_ff1, w_ff2):
    depth = w_in.shape[0]
    p_layers = jax.nn.softmax(hg_lb.astype(_F32), axis=0)
    lower_bounds = jnp.cumsum(p_layers, axis=0) - p_layers[0]
    zeros_row = jnp.zeros((HEAD_DIM,), _F32)
    y_prompt, y_sample = x_prompt, x_sample
    for l in range(depth):
        p = {
            "norm_mix_pre": norm_mix_pre[l].reshape(1, D_MODEL),
            "norm_mix_post": norm_mix_post[l].reshape(1, D_MODEL),
            "norm_ffn_pre": norm_ffn_pre[l].reshape(1, D_MODEL),
            "norm_ffn_post": norm_ffn_post[l].reshape(1, D_MODEL),
            "w_in": _permute_w_in(w_in[l]),
            "dn_conv": dn_conv[l],
            "dn_gates": jnp.stack([_lane_row(dn_dt_bias[l], SM_ALPHA), _lane_row(dn_a_log[l], SM_ALPHA)]
                                  + [zeros_row] * 6),
            "hg_lb": lower_bounds[l],
            "ml_conv": ml_conv[l],
            "ml_gates": jnp.stack([_lane_row(ml_i_bias[l], SM_IG), _lane_row(ml_f_bias[l], SM_FG)]
                                  + [zeros_row] * 6),
            "mix_norm": jnp.stack([dn_norm[l].reshape(-1), hg_norm[l].reshape(-1), ml_norm[l].reshape(-1)]
                                  + [jnp.zeros((MIX_WIDTH,), _F32)] * 5),
            "w_branch": w_branch[l].astype(_BF16),
            "w_out": w_out[l].astype(_BF16),
            "w_ff1": w_ff1[l].astype(_BF16),
            "w_ff2": w_ff2[l].astype(_BF16),
        }
        y_prompt = _layer(y_prompt, p)
        y_sample = _layer(y_sample, p)
    return (y_prompt, y_sample)
```

```python
import jax
import jax.numpy as jnp
from jax import lax
from jax.experimental import pallas as pl
from jax.experimental.pallas import tpu as pltpu

D_MODEL = 1024
N_HEADS = 4
HEAD_DIM = 128
MIX_WIDTH = N_HEADS * HEAD_DIM
N_BRANCH = 3
CONV_K = 5
CONV_HALO = 8
D_FF = 4 * D_MODEL
EPS = 1e-6
NEG_BIG = -1e30
CHUNK = 128
SUB = 4
SOLVE_BASE = 16

GP_OFF = 0
DN_QKV_OFF = 3072
DN_Z_OFF = 4608
ML_QK_OFF = 5120
ML_V_OFF = 6144
ML_O_OFF = 6656
HG_Q_OFF = 7168
HG_F_OFF = 7680
HG_I_OFF = 8704
HG_Z_OFF = 9216
SM_OFF = 9728
N_COLS = 10240
SM_BETA, SM_ALPHA, SM_IG, SM_FG = 0, 8, 16, 24
DN_ACT_W = 3 * MIX_WIDTH
ML_ACT_W = 2 * MIX_WIDTH

PROJ_TM = 1024
PROJ_TN = 2048
PREP_TM = 256
MERGE_TM = 256
FFN_TM = 512
VMEM_LIMIT = 56 * 1024 * 1024

_F32 = jnp.float32
_BF16 = jnp.bfloat16
_CHAINS = tuple((d, h) for d in range(2) for h in range(N_HEADS))
_DONE = object()


def _bdot(a, b):
    return jnp.dot(a.astype(_BF16), b.astype(_BF16), preferred_element_type=_F32)


def _bdot_nt(a, b):
    return lax.dot_general(a.astype(_BF16), b.astype(_BF16), (((1,), (1,)), ((), ())),
                           preferred_element_type=_F32)


def _bdot_tn(a, b):
    return lax.dot_general(a.astype(_BF16), b.astype(_BF16), (((0,), (0,)), ((), ())),
                           preferred_element_type=_F32)


def _sigmoid(x):
    return 0.5 * jnp.tanh(0.5 * x) + 0.5


def _silu(x):
    h = 0.5 * x
    return h * jnp.tanh(h) + h


def _softplus(x):
    return jnp.maximum(x, 0.0) + jnp.log(1.0 + jnp.exp(-jnp.abs(x)))


def _rmsnorm(x, g):
    return x * lax.rsqrt(jnp.mean(x * x, axis=-1, keepdims=True) + EPS) * g


def _cumsum_rows(x, rev):
    c, w = x.shape
    row = lax.broadcasted_iota(jnp.int32, (c, c), 0)
    col = lax.broadcasted_iota(jnp.int32, (c, c), 1)
    tri = jnp.where((col >= row) if rev else (col <= row), 1.0, 0.0).astype(_BF16)
    hi = x.astype(_BF16)
    r1 = x - hi.astype(_F32)
    mid = r1.astype(_BF16)
    lo = (r1 - mid.astype(_F32)).astype(_BF16)
    s = jnp.dot(tri, jnp.concatenate([hi, mid, lo], axis=1), preferred_element_type=_F32)
    return s[:, :w] + s[:, w:2 * w] + s[:, 2 * w:]


def _head(t, h):
    return t[:, h * HEAD_DIM:(h + 1) * HEAD_DIM]


def _proj_kernel(x_ref, g_ref, w_ref, o_ref, h_scr):
    @pl.when(pl.program_id(1) == 0)
    def _():
        h_scr[...] = _rmsnorm(x_ref[...], g_ref[...]).astype(_BF16)

    o_ref[...] = jnp.dot(h_scr[...], w_ref[...], preferred_element_type=_F32)


def _proj(x2, g, w):
    t = x2.shape[0]
    tm = min(PROJ_TM, t)
    return pl.pallas_call(
        _proj_kernel,
        grid=(t // tm, N_COLS // PROJ_TN),
        in_specs=[pl.BlockSpec((tm, D_MODEL), lambda i, j: (i, 0)),
                  pl.BlockSpec((1, D_MODEL), lambda i, j: (0, 0)),
                  pl.BlockSpec((D_MODEL, PROJ_TN), lambda i, j: (0, j))],
        out_specs=pl.BlockSpec((tm, PROJ_TN), lambda i, j: (i, j)),
        out_shape=jax.ShapeDtypeStruct((t, N_COLS), _F32),
        scratch_shapes=[pltpu.VMEM((tm, D_MODEL), _BF16)],
        compiler_params=pltpu.CompilerParams(dimension_semantics=("arbitrary", "arbitrary"),
                                             vmem_limit_bytes=VMEM_LIMIT),
        name="proj",
    )(x2, g, w)


def _conv_act(prev8, cur, next8, w, first, last):
    c = cur.shape[0]
    prev8 = jnp.where(first, 0.0, prev8)
    next8 = jnp.where(last, 0.0, next8)
    ext = jnp.concatenate([prev8, cur, next8], axis=0)
    rows = c + 2 * CONV_HALO
    y = None
    for j in range(CONV_K):
        sh = (CONV_K // 2 - j) % rows
        tap = ext if sh == 0 else pltpu.roll(ext, sh, 0)
        term = w[j:j + 1] * tap[CONV_HALO:CONV_HALO + c]
        y = term if y is None else y + term
    return _silu(y)


def _prep_kernel(dq, dp, dn, mq, mp, mn, dw_ref, mw_ref, dn_out, ml_out):
    n = pl.program_id(1)
    first = n == 0
    last = n == pl.num_programs(1) - 1
    x = _conv_act(dp[0], dq[0], dn[0], dw_ref[...], first, last)
    parts = []
    for h in range(2 * N_HEADS):
        t = _head(x, h)
        t = t * lax.rsqrt(jnp.sum(t * t, axis=-1, keepdims=True) + EPS)
        parts.append(t * (HEAD_DIM ** -0.5) if h < N_HEADS else t)
    parts.append(x[:, 2 * MIX_WIDTH:])
    dn_out[0] = jnp.concatenate(parts, axis=1).astype(_BF16)
    y = _conv_act(mp[0], mq[0], mn[0], mw_ref[...], first, last)
    ml_out[0] = jnp.concatenate([y[:, :MIX_WIDTH] * (HEAD_DIM ** -0.5), y[:, MIX_WIDTH:]], axis=1).astype(_BF16)


def _halo_specs(tm, seq, width, off):
    blk = off // width
    r8 = tm // CONV_HALO
    last8 = seq // CONV_HALO - 1
    return [pl.BlockSpec((1, tm, width), lambda b, n: (b, n, blk)),
            pl.BlockSpec((1, CONV_HALO, width), lambda b, n: (b, jnp.maximum(n * r8 - 1, 0), blk)),
            pl.BlockSpec((1, CONV_HALO, width), lambda b, n: (b, jnp.minimum((n + 1) * r8, last8), blk))]


def _prep_call(zin3, dn_conv, ml_conv):
    b, s, _ = zin3.shape
    tm = min(PREP_TM, s)
    in_specs = (_halo_specs(tm, s, DN_ACT_W, DN_QKV_OFF) + _halo_specs(tm, s, ML_ACT_W, ML_QK_OFF)
                + [_const_spec(dn_conv.shape), _const_spec(ml_conv.shape)])
    return pl.pallas_call(
        _prep_kernel,
        grid=(b, s // tm),
        in_specs=in_specs,
        out_specs=[pl.BlockSpec((1, tm, DN_ACT_W), lambda b, n: (b, n, 0)),
                   pl.BlockSpec((1, tm, ML_ACT_W), lambda b, n: (b, n, 0))],
        out_shape=[jax.ShapeDtypeStruct((b, s, DN_ACT_W), _BF16),
                   jax.ShapeDtypeStruct((b, s, ML_ACT_W), _BF16)],
        compiler_params=pltpu.CompilerParams(dimension_semantics=("arbitrary", "arbitrary"),
                                             vmem_limit_bytes=VMEM_LIMIT),
        name="prep",
    )(zin3, zin3, zin3, zin3, zin3, zin3, dn_conv, ml_conv)


def _pair_specs(c, n_chunks, width, off):
    blk = off // width
    return [pl.BlockSpec((1, c, width), lambda b, n: (b, n, blk)),
            pl.BlockSpec((1, c, width), lambda b, n: (b, n_chunks - 1 - n, blk))]


def _const_spec(shape):
    nd = len(shape)
    return pl.BlockSpec(shape, lambda b, n: (0,) * nd)


def _tri_masks(c, rev):
    row = lax.broadcasted_iota(jnp.int32, (c, c), 0)
    col = lax.broadcasted_iota(jnp.int32, (c, c), 1)
    return ((col >= row), (col > row)) if rev else ((col <= row), (col < row))


def _unit_tri_inverse_many(mats):
    c = mats[0].shape[0]
    row = lax.broadcasted_iota(jnp.int32, (c, c), 0)
    col = lax.broadcasted_iota(jnp.int32, (c, c), 1)
    blk = min(SOLVE_BASE, c)
    eye = jnp.where(row == col, 1.0, 0.0).astype(_F32)
    diag_blk = (row // blk) == (col // blk)
    a0 = [jnp.where(diag_blk, a, 0.0) for a in mats]
    ts = [eye - x for x in a0]
    pws = [_bdot(x, x) for x in a0]
    yield
    k = 2
    while 2 * k < blk:
        boths = [_bdot(jnp.concatenate([t, pw], axis=0), pw) for t, pw in zip(ts, pws)]
        ts = [t + both[:c] for t, both in zip(ts, boths)]
        pws = [both[c:] for both in boths]
        k *= 2
        yield
    ts = [t + _bdot(t, pw) for t, pw in zip(ts, pws)]
    yield
    while blk < c:
        off = ((row // (2 * blk)) == (col // (2 * blk))) & ((row // blk) != (col // blk))
        tl = [_bdot(t, jnp.where(off, a, 0.0)) for t, a in zip(ts, mats)]
        yield
        ts = [t - _bdot(x, t) for t, x in zip(ts, tl)]
        yield
        blk *= 2
    return ts


def _dn_body(af, sf, ab, sb, gp_ref, of_ref, ob_ref, s_ref):
    c = af.shape[1]
    dt_bias = gp_ref[0:1, :]
    neg_a = -jnp.exp(gp_ref[1:2, :])
    gates = []
    for rev, sm_ref in ((False, sf), (True, sb)):
        sm = sm_ref[0]
        g_t = neg_a * _softplus(sm + dt_bias)
        gcum = _cumsum_rows(g_t, rev)
        gtot = jnp.sum(g_t, axis=0, keepdims=True)
        gates.append(dict(beta=_sigmoid(sm), gcum=gcum, gcum_t=gcum.T, e_g=jnp.exp(gcum),
                          e_rest=jnp.exp(gtot - gcum), e_tot=jnp.exp(gtot), masks=_tri_masks(c, rev)))
    yield
    acts = (af, ab)
    q, k, kb, vb, eg, a_low, attn = [], [], [], [], [], [], []
    for d, h in _CHAINS:
        g = gates[d]
        lane_a = SM_ALPHA + d * N_HEADS + h
        lane_b = SM_BETA + d * N_HEADS + h
        incl, strict = g["masks"]
        qh = acts[d][0, :, h * HEAD_DIM:(h + 1) * HEAD_DIM]
        kh = acts[d][0, :, MIX_WIDTH + h * HEAD_DIM:MIX_WIDTH + (h + 1) * HEAD_DIM]
        vh = acts[d][0, :, 2 * MIX_WIDTH + h * HEAD_DIM:2 * MIX_WIDTH + (h + 1) * HEAD_DIM]
        beta = g["beta"][:, lane_b:lane_b + 1]
        decay = jnp.exp(jnp.where(incl, g["gcum"][:, lane_a:lane_a + 1] - g["gcum_t"][lane_a:lane_a + 1, :],
                                  NEG_BIG))
        kbh = kh.astype(_F32) * beta
        both = _bdot_nt(jnp.concatenate([kbh.astype(_BF16), qh], axis=0), kh)
        a_low.append(jnp.where(strict, both[:c] * decay, 0.0))
        attn.append(both[c:] * decay)
        q.append(qh)
        k.append(kh)
        kb.append(kbh)
        vb.append(vh.astype(_F32) * beta)
        eg.append(g["e_g"][:, lane_a:lane_a + 1])
        if h == N_HEADS - 1:
            yield
    tinv = yield from _unit_tri_inverse_many(a_low)
    sol = [_bdot(t, jnp.concatenate([vb[i], kb[i] * eg[i]], axis=1)) for i, t in enumerate(tinv)]
    yield
    ws_qs = [_bdot(jnp.concatenate([sol[i][:, HEAD_DIM:], q[i].astype(_F32) * eg[i]], axis=0), s_ref[d, h])
             for i, (d, h) in enumerate(_CHAINS)]
    yield
    v_new = [sol[i][:, :HEAD_DIM] - ws_qs[i][:c] for i in range(len(_CHAINS))]
    outs = [ws_qs[i][c:] + _bdot(attn[i], v_new[i]) for i in range(len(_CHAINS))]
    yield
    for i, (d, h) in enumerate(_CHAINS):
        lane_a = SM_ALPHA + d * N_HEADS + h
        er = gates[d]["e_rest"][:, lane_a:lane_a + 1]
        et = gates[d]["e_tot"][:, lane_a:lane_a + 1]
        s_ref[d, h] = s_ref[d, h] * et + _bdot_tn(k[i].astype(_F32) * er, v_new[i])
    of_ref[0] = jnp.concatenate(outs[:N_HEADS], axis=1).astype(_BF16)
    ob_ref[0] = jnp.concatenate(outs[N_HEADS:], axis=1).astype(_BF16)


def _hg_level_masks(c, rev):
    row = lax.broadcasted_iota(jnp.int32, (c, c), 0)
    col = lax.broadcasted_iota(jnp.int32, (c, c), 1)
    r1 = lax.broadcasted_iota(jnp.int32, (c, 1), 0)
    levels = []
    blk = c // 2
    while blk >= SUB:
        same_pair = (row // (2 * blk)) == (col // (2 * blk))
        row_hi = (row // blk) % 2 == 1
        col_hi = (col // blk) % 2 == 1
        if rev:
            mask = same_pair & (~row_hi) & col_hi
            q_rows = (r1 // blk) % 2 == 0
        else:
            mask = same_pair & row_hi & (~col_hi)
            q_rows = (r1 // blk) % 2 == 1
        levels.append((blk, mask, q_rows))
        blk //= 2
    return levels


def _hg_band_masks(c, rev):
    row = lax.broadcasted_iota(jnp.int32, (c, c), 0)
    col = lax.broadcasted_iota(jnp.int32, (c, c), 1)
    masks = []
    for dl in range(SUB):
        if rev:
            masks.append((col == row + dl) & ((row % SUB) + dl < SUB))
        else:
            masks.append((col == row - dl) & ((row % SUB) >= dl))
    return masks


def _pair_reference(b, blk, rev):
    c, w = b.shape
    b3 = b.reshape(c // (2 * blk), 2 * blk, w)
    ref = b3[:, blk:blk + 1, :] if rev else b3[:, blk - 1:blk, :]
    return jnp.broadcast_to(ref, b3.shape).reshape(c, w)


def _hg_body(qf, ff, vf, qb, fb, vb, lb_ref, of_ref, ob_ref, s_ref):
    c = qf.shape[1]
    for d, (q_ref, f_ref, v_ref, o_ref) in enumerate(((qf, ff, vf, of_ref), (qb, fb, vb, ob_ref))):
        rev = d == 1
        lb = lb_ref[d:d + 1, :]
        sig = _sigmoid(f_ref[0])
        q_all = _silu(q_ref[0])
        k_all = (1.0 - lb) * (1.0 - sig)
        f_all = lb + (1.0 - lb) * sig
        logf = jnp.log(f_all)
        b_all = _cumsum_rows(logf, rev)
        btot = jnp.sum(logf, axis=0, keepdims=True)
        qe_all = q_all * jnp.exp(b_all)
        ke_all = k_all * jnp.exp(btot - b_all)
        e_tot = jnp.exp(btot)
        v_all = v_ref[0]
        levels = _hg_level_masks(c, rev)
        bands = _hg_band_masks(c, rev)
        yield
        outs = []
        for h in range(N_HEADS):
            q = _head(q_all, h)
            k = _head(k_all, h)
            v = _head(v_all, h)
            b = _head(b_all, h)
            attn = jnp.zeros((c, c), _F32)
            for blk, mask, q_rows in levels:
                ref = _pair_reference(b, blk, rev)
                e = jnp.exp(jnp.where(q_rows, b - ref, ref - b))
                qs = jnp.where(q_rows, q * e, 0.0)
                ks = jnp.where(q_rows, 0.0, k * e)
                attn = attn + jnp.where(mask, _bdot_nt(qs, ks), 0.0)
            f = _head(f_all, h)
            qd = q
            for dl in range(SUB):
                if dl > 0:
                    qd = qd * (f if dl == 1 else pltpu.roll(f, ((c - dl + 1) if rev else (dl - 1)), 0))
                k_dl = k if dl == 0 else pltpu.roll(k, ((c - dl) if rev else dl), 0)
                s_dl = jnp.sum(qd * k_dl, axis=-1, keepdims=True)
                attn = jnp.where(bands[dl], s_dl, attn)
            state_t = s_ref[d, h]
            outs.append(_bdot_nt(_head(qe_all, h), state_t) + _bdot(attn, v))
            s_ref[d, h] = state_t * _head(e_tot, h) + _bdot_tn(v, _head(ke_all, h))
            yield
        o_ref[0] = jnp.concatenate(outs, axis=1).astype(_BF16)


def _ml_body(af, vf, sf, ab, vb, sb, gp_ref, of_ref, ob_ref, c_ref, m_ref):
    c = af.shape[1]
    lane = lax.broadcasted_iota(jnp.int32, (c, HEAD_DIM), 1)
    ones_col = jnp.where(lane == 0, 1.0, 0.0).astype(_BF16)
    gates = []
    for rev, sm_ref in ((False, sf), (True, sb)):
        sm = sm_ref[0]
        ig_t = sm + gp_ref[0:1, :]
        f_t = sm + gp_ref[1:2, :]
        logf = jnp.minimum(f_t, 0.0) - jnp.log(1.0 + jnp.exp(-jnp.abs(f_t)))
        bcum = _cumsum_rows(logf, rev)
        comb_t = jnp.where((lane >= SM_FG) & (lane < SM_FG + 2 * N_HEADS), bcum, ig_t).T
        gates.append(dict(ig=ig_t, bcum=bcum, btot=jnp.sum(logf, axis=0, keepdims=True), comb_t=comb_t,
                          incl=_tri_masks(c, rev)[0]))
    acts = (af, ab)
    vals = (vf, vb)
    q, k, v_ext, qk, qs = [], [], [], [], []
    for d, h in _CHAINS:
        qh = acts[d][0, :, h * HEAD_DIM:(h + 1) * HEAD_DIM]
        kh = acts[d][0, :, MIX_WIDTH + h * HEAD_DIM:MIX_WIDTH + (h + 1) * HEAD_DIM]
        vh = vals[d][0, :, h * HEAD_DIM:(h + 1) * HEAD_DIM]
        q.append(qh)
        k.append(kh)
        v_ext.append(jnp.concatenate([vh.astype(_BF16), ones_col], axis=1))
        qk.append(_bdot_nt(qh, kh))
        qs.append(_bdot(qh, c_ref[d, h]))
        if h == N_HEADS - 1:
            yield
    n_ch = len(_CHAINS)
    logw, inter, logw_end, bt_m = [], [], [], []
    for d, h in _CHAINS:
        g = gates[d]
        col = d * N_HEADS + h
        b_col = g["bcum"][:, SM_FG + col:SM_FG + col + 1]
        i_col = g["ig"][:, SM_IG + col:SM_IG + col + 1]
        b_row = g["comb_t"][SM_FG + col:SM_FG + col + 1, :]
        i_row = g["comb_t"][SM_IG + col:SM_IG + col + 1, :]
        bt = g["btot"][:, SM_FG + col:SM_FG + col + 1]
        m_old = m_ref[col:col + 1, 0:1]
        logw.append(jnp.where(g["incl"], b_col - b_row + i_row, NEG_BIG))
        inter.append(b_col + m_old)
        logw_end.append(bt - b_col + i_col)
        bt_m.append(bt + m_old)
    yield
    m_row = [jnp.maximum(inter[i], jnp.max(logw[i], axis=-1, keepdims=True)) for i in range(n_ch)]
    m_new = [jnp.maximum(bt_m[i], jnp.max(logw_end[i], axis=0, keepdims=True)) for i in range(n_ch)]
    yield
    p = [qk[i] * jnp.exp(logw[i] - m_row[i]) for i in range(n_ch)]
    yield
    pv = [_bdot(p[i], v_ext[i]) for i in range(n_ch)]
    yield
    outs = []
    for i in range(n_ch):
        nd = jnp.exp(inter[i] - m_row[i]) * qs[i] + pv[i]
        den = nd[:, HEAD_DIM:HEAD_DIM + 1]
        outs.append(nd[:, :HEAD_DIM] / jnp.maximum(jnp.abs(den), jnp.exp(-m_row[i])))
    yield
    kw = [k[i].astype(_F32) * jnp.exp(logw_end[i] - m_new[i]) for i in range(n_ch)]
    for i, (d, h) in enumerate(_CHAINS):
        col = d * N_HEADS + h
        c_ref[d, h] = jnp.exp(bt_m[i] - m_new[i]) * c_ref[d, h] + _bdot_tn(kw[i], v_ext[i])
        m_ref[col:col + 1, :] = jnp.broadcast_to(m_new[i], (1, HEAD_DIM))
    of_ref[0] = jnp.concatenate(outs[:N_HEADS], axis=1).astype(_BF16)
    ob_ref[0] = jnp.concatenate(outs[N_HEADS:], axis=1).astype(_BF16)


def _mix_kernel(sf, sb, dn_af, dn_ab, dn_gp, hg_qf, hg_ff, hg_vf, hg_qb, hg_fb, hg_vb, hg_lb,
                ml_af, ml_vf, ml_ab, ml_vb, ml_gp,
                dn_of, dn_ob, hg_of, hg_ob, ml_of, ml_ob, dn_s, hg_s, ml_c, ml_m):
    @pl.when(pl.program_id(1) == 0)
    def _():
        dn_s[...] = jnp.zeros_like(dn_s)
        hg_s[...] = jnp.zeros_like(hg_s)
        ml_c[...] = jnp.zeros_like(ml_c)
        ml_m[...] = jnp.full(ml_m.shape, NEG_BIG, _F32)

    dn = _dn_body(dn_af, sf, dn_ab, sb, dn_gp, dn_of, dn_ob, dn_s)
    hg = _hg_body(hg_qf, hg_ff, hg_vf, hg_qb, hg_fb, hg_vb, hg_lb, hg_of, hg_ob, hg_s)
    ml = _ml_body(ml_af, ml_vf, sf, ml_ab, ml_vb, sb, ml_gp, ml_of, ml_ob, ml_c, ml_m)
    pending = [dn, hg, dn, ml]
    while pending:
        for gen in list(pending):
            if gen in pending and next(gen, _DONE) is _DONE:
                pending = [g for g in pending if g is not gen]


def _mix_call(zin3, dn_act, ml_act, dn_gates, hg_lb, ml_gates):
    b, s, _ = zin3.shape
    c = min(CHUNK, s)
    n_chunks = s // c
    pair = lambda width, off: _pair_specs(c, n_chunks, width, off)
    sm, dn_a, ml_a = pair(HEAD_DIM, SM_OFF), pair(DN_ACT_W, 0), pair(ML_ACT_W, 0)
    hg_q, hg_v, ml_v = pair(MIX_WIDTH, HG_Q_OFF), pair(MIX_WIDTH, HG_I_OFF), pair(MIX_WIDTH, ML_V_OFF)
    hg_ff = pair(MIX_WIDTH, HG_F_OFF)[0]
    hg_fb = pair(MIX_WIDTH, HG_F_OFF + MIX_WIDTH)[1]
    in_specs = [sm[0], sm[1], dn_a[0], dn_a[1], _const_spec(dn_gates.shape),
                hg_q[0], hg_ff, hg_v[0], hg_q[1], hg_fb, hg_v[1], _const_spec(hg_lb.shape),
                ml_a[0], ml_v[0], ml_a[1], ml_v[1], _const_spec(ml_gates.shape)]
    args = (zin3, zin3, dn_act, dn_act, dn_gates, zin3, zin3, zin3, zin3, zin3, zin3, hg_lb,
            ml_act, zin3, ml_act, zin3, ml_gates)
    out = jax.ShapeDtypeStruct((b, s, MIX_WIDTH), _BF16)
    state = (2, N_HEADS, HEAD_DIM, HEAD_DIM)
    return pl.pallas_call(
        _mix_kernel,
        grid=(b, n_chunks),
        in_specs=in_specs,
        out_specs=pair(MIX_WIDTH, 0) * 3,
        out_shape=[out] * 6,
        scratch_shapes=[pltpu.VMEM(state, _F32), pltpu.VMEM(state, _F32),
                        pltpu.VMEM((2, N_HEADS, HEAD_DIM, 2 * HEAD_DIM), _F32),
                        pltpu.VMEM((2 * N_HEADS, HEAD_DIM), _F32)],
        compiler_params=pltpu.CompilerParams(dimension_semantics=("arbitrary", "arbitrary"),
                                             vmem_limit_bytes=VMEM_LIMIT),
        name="mixers",
    )(*args)


def _head_norm(o, w):
    parts = []
    for h in range(N_HEADS):
        seg = _head(o, h)
        parts.append(seg * lax.rsqrt(jnp.mean(seg * seg, axis=-1, keepdims=True) + EPS))
    return jnp.concatenate(parts, axis=1) * w


def _merge_kernel(x_ref, dnf, dnb, hgf, hgb, mlf, mlb, dnz, hgz, mlo, gp_ref,
                  nw_ref, wbr_ref, wout_ref, gpost_ref, o_ref):
    both = lambda f, b: f[...].astype(_F32) + b[...].astype(_F32)
    branches = (
        _head_norm(both(dnf, dnb), nw_ref[0:1, :]) * _silu(dnz[...]),
        _head_norm(both(hgf, hgb), nw_ref[1:2, :]) * _silu(hgz[...]),
        _head_norm(both(mlf, mlb), nw_ref[2:3, :]) * _sigmoid(mlo[...]),
    )
    merged = None
    for i, br in enumerate(branches):
        gate = _sigmoid(gp_ref[:, i * D_MODEL:(i + 1) * D_MODEL])
        term = gate * _bdot(br, wbr_ref[i])
        merged = term if merged is None else merged + term
    y = _bdot(merged, wout_ref[...])
    o_ref[...] = x_ref[...] + _rmsnorm(y, gpost_ref[...])


def _merge_call(x2, zin2, mix_outs, norm_w, w_branch, w_out, g_post):
    t = x2.shape[0]
    tm = min(MERGE_TM, t)
    row = lambda w: pl.BlockSpec((tm, w), lambda i: (i, 0))
    zblk = lambda w, off: pl.BlockSpec((tm, w), lambda i: (i, off // w))
    full = lambda a: pl.BlockSpec(a.shape, lambda i: (0,) * a.ndim)
    in_specs = ([row(D_MODEL)] + [row(MIX_WIDTH)] * 6
                + [zblk(MIX_WIDTH, DN_Z_OFF), zblk(MIX_WIDTH, HG_Z_OFF), zblk(MIX_WIDTH, ML_O_OFF),
                   zblk(N_BRANCH * D_MODEL, GP_OFF)]
                + [full(norm_w), full(w_branch), full(w_out), full(g_post)])
    return pl.pallas_call(
        _merge_kernel,
        grid=(t // tm,),
        in_specs=in_specs,
        out_specs=row(D_MODEL),
        out_shape=jax.ShapeDtypeStruct((t, D_MODEL), _F32),
        compiler_params=pltpu.CompilerParams(dimension_semantics=("arbitrary",),
                                             vmem_limit_bytes=VMEM_LIMIT),
        name="merge",
    )(x2, *mix_outs, zin2, zin2, zin2, zin2, norm_w, w_branch, w_out, g_post)


def _ffn_kernel(x_ref, gpre_ref, w1_ref, w2_ref, gpost_ref, o_ref):
    x = x_ref[...]
    u = jnp.maximum(_bdot(_rmsnorm(x, gpre_ref[...]), w1_ref[...]), 0.0)
    y = _bdot(u * u, w2_ref[...])
    o_ref[...] = x + _rmsnorm(y, gpost_ref[...])


def _ffn_call(x2, g_pre, w1, w2, g_post):
    t = x2.shape[0]
    tm = min(FFN_TM, t)
    row = pl.BlockSpec((tm, D_MODEL), lambda i: (i, 0))
    full = lambda a: pl.BlockSpec(a.shape, lambda i: (0,) * a.ndim, pipeline_mode=pl.Buffered(1))
    return pl.pallas_call(
        _ffn_kernel,
        grid=(t // tm,),
        in_specs=[row, full(g_pre), full(w1), full(w2), full(g_post)],
        out_specs=row,
        out_shape=jax.ShapeDtypeStruct((t, D_MODEL), _F32),
        compiler_params=pltpu.CompilerParams(dimension_semantics=("arbitrary",),
                                             vmem_limit_bytes=VMEM_LIMIT),
        name="ffn",
    )(x2, g_pre, w1, w2, g_post)


def _permute_w_in(w_in):
    m = MIX_WIDTH
    o_dn, o_sm_dn, o_hg, o_ml, o_sm_ml, o_gp = 0, 4 * m, 4 * m + 16, 9 * m + 16, 13 * m + 16, 13 * m + 32
    parts = [
        w_in[:, o_gp:o_gp + N_BRANCH * D_MODEL],
        w_in[:, o_dn:o_dn + 4 * m],
        w_in[:, o_ml:o_ml + 4 * m],
        w_in[:, o_hg:o_hg + 5 * m],
        w_in[:, o_sm_dn:o_sm_dn + 16],
        w_in[:, o_sm_ml:o_sm_ml + 16],
        jnp.zeros((w_in.shape[0], N_COLS - SM_OFF - 32), w_in.dtype),
    ]
    return jnp.concatenate(parts, axis=1).astype(_BF16)


def _lane_row(values, off):
    row = jnp.zeros((HEAD_DIM,), _F32)
    return row.at[off:off + 2 * N_HEADS].set(values.astype(_F32).reshape(-1))


def _layer(x3, p):
    b, s, _ = x3.shape
    x2 = x3.reshape(b * s, D_MODEL)
    zin2 = _proj(x2, p["norm_mix_pre"], p["w_in"])
    zin3 = zin2.reshape(b, s, N_COLS)
    dn_act, ml_act = _prep_call(zin3, p["dn_conv"], p["ml_conv"])
    mix = _mix_call(zin3, dn_act, ml_act, p["dn_gates"], p["hg_lb"], p["ml_gates"])
    mix = [a.reshape(b * s, MIX_WIDTH) for a in mix]
    x2 = _merge_call(x2, zin2, mix, p["mix_norm"], p["w_branch"], p["w_out"], p["norm_mix_post"])
    x2 = _ffn_call(x2, p["norm_ffn_pre"], p["w_ff1"], p["w_ff2"], p["norm_ffn_post"])
    return x2.reshape(b, s, D_MODEL)


def kernel(x_prompt, x_sample, norm_mix_pre, norm_mix_post, norm_ffn_pre, norm_ffn_post, w_in, dn_conv,
           dn_a_log, dn_dt_bias, dn_norm, hg_lb, hg_norm, ml_conv, ml_i_bias, ml_f_bias, ml_norm,
           w_branch, w_out, w_ff1, w_ff2):
    depth = w_in.shape[0]
    p_layers = jax.nn.softmax(hg_lb.astype(_F32), axis=0)
    lower_bounds = jnp.cumsum(p_layers, axis=0) - p_layers[0]
    zeros_row = jnp.zeros((HEAD_DIM,), _F32)
    y_prompt, y_sample = x_prompt, x_sample
    for l in range(depth):
        p = {
            "norm_mix_pre": norm_mix_pre[l].reshape(1, D_MODEL),
            "norm_mix_post": norm_mix_post[l].reshape(1, D_MODEL),
            "norm_ffn_pre": norm_ffn_pre[l].reshape(1, D_MODEL),
            "norm_ffn_post": norm_ffn_post[l].reshape(1, D_MODEL),
            "w_in": _permute_w_in(w_in[l]),
            "dn_conv": dn_conv[l],
            "dn_gates": jnp.stack([_lane_row(dn_dt_bias[l], SM_ALPHA), _lane_row(dn_a_log[l], SM_ALPHA)]
                                  + [zeros_row] * 6),
            "hg_lb": lower_bounds[l],
            "ml_conv": ml_conv[l],
            "ml_gates": jnp.stack([_lane_row(ml_i_bias[l], SM_IG), _lane_row(ml_f_bias[l], SM_FG)]
                                  + [zeros_row] * 6),
            "mix_norm": jnp.stack([dn_norm[l].reshape(-1), hg_norm[l].reshape(-1), ml_norm[l].reshape(-1)]
                                  + [jnp.zeros((MIX_WIDTH,), _F32)] * 5),
            "w_branch": w_branch[l].astype(_BF16),
            "w_out": w_out[l].astype(_BF16),
            "w_ff1": w_ff1[l].astype(_BF16),
            "w_ff2": w_ff2[l].astype(_BF16),
        }
        y_prompt = _layer(y_prompt, p)
        y_sample = _layer(y_sample, p)
    return (y_prompt, y_sample)
```

```python
import jax
import jax.numpy as jnp
from jax import lax
from jax.experimental import pallas as pl
from jax.experimental.pallas import tpu as pltpu

D_MODEL = 1024
N_HEADS = 4
HEAD_DIM = 128
MIX_WIDTH = N_HEADS * HEAD_DIM
N_BRANCH = 3
CONV_K = 5
CONV_HALO = 8
HALO_ROWS = 16
D_FF = 4 * D_MODEL
EPS = 1e-6
LOG2_E = 1.4426950408889634
NEG_BIG = -1e30
CHUNK = 128
SUB = 4
SOLVE_BASE = 16

GP_OFF = 0
DN_QKV_OFF = 3072
DN_Z_OFF = 4608
ML_QK_OFF = 5120
ML_V_OFF = 6144
ML_O_OFF = 6656
HG_Q_OFF = 7168
HG_I_OFF = 7680
HG_Z_OFF = 8192
NB_COLS = 8960
HG_F_OFF = 0
SM_OFF = 1024
NF_COLS = 1152
SM_BETA, SM_ALPHA, SM_IG, SM_FG = 0, 8, 16, 24
DN_ACT_W = 3 * MIX_WIDTH
ML_ACT_W = 2 * MIX_WIDTH

PROJ_TM = 1024
PROJ_TN = 1792
PREP_TM = 256
MERGE_TM = 256
FFN_TM = 512
VMEM_LIMIT = 56 * 1024 * 1024

_F32 = jnp.float32
_BF16 = jnp.bfloat16
_CHAINS = tuple((d, h) for d in range(2) for h in range(N_HEADS))
_DONE = object()


def _bdot(a, b):
    return jnp.dot(a.astype(_BF16), b.astype(_BF16), preferred_element_type=_F32)


def _bdot_nt(a, b):
    return lax.dot_general(a.astype(_BF16), b.astype(_BF16), (((1,), (1,)), ((), ())),
                           preferred_element_type=_F32)


def _bdot_tn(a, b):
    return lax.dot_general(a.astype(_BF16), b.astype(_BF16), (((0,), (0,)), ((), ())),
                           preferred_element_type=_F32)


def _sigmoid(x):
    return 0.5 * jnp.tanh(0.5 * x) + 0.5


def _silu(x):
    h = 0.5 * x
    return h * jnp.tanh(h) + h


def _softplus(x):
    return jnp.maximum(x, 0.0) + jnp.log(1.0 + jnp.exp(-jnp.abs(x)))


def _rmsnorm(x, g):
    return x * lax.rsqrt(jnp.mean(x * x, axis=-1, keepdims=True) + EPS) * g


def _split3(x):
    mask = jnp.uint32(0xFFFF0000)

    def top(v):
        return lax.bitcast_convert_type(lax.bitcast_convert_type(v, jnp.uint32) & mask, _F32)

    hi = top(x)
    r1 = x - hi
    mid = top(r1)
    return hi.astype(_BF16), mid.astype(_BF16), (r1 - mid).astype(_BF16)


def _cumsum_rows(x, rev):
    c, w = x.shape
    row = lax.broadcasted_iota(jnp.int32, (c, c), 0)
    col = lax.broadcasted_iota(jnp.int32, (c, c), 1)
    tri = jnp.where((col >= row) if rev else (col <= row), 1.0, 0.0).astype(_BF16)
    s = jnp.dot(tri, jnp.concatenate(_split3(x), axis=1), preferred_element_type=_F32)
    return s[:, :w] + s[:, w:2 * w] + s[:, 2 * w:]


def _cumsum_lanes(x, rev):
    r, c = x.shape
    row = lax.broadcasted_iota(jnp.int32, (c, c), 0)
    col = lax.broadcasted_iota(jnp.int32, (c, c), 1)
    tri = jnp.where((row >= col) if rev else (row <= col), 1.0, 0.0).astype(_BF16)
    s = jnp.dot(jnp.concatenate(_split3(x), axis=0), tri, preferred_element_type=_F32)
    return s[:r] + s[r:2 * r] + s[2 * r:]


def _head(t, h):
    return t[:, h * HEAD_DIM:(h + 1) * HEAD_DIM]


def _proj_kernel(x_ref, g_ref, w_ref, o_ref, h_scr):
    @pl.when(pl.program_id(1) == 0)
    def _():
        h_scr[...] = _rmsnorm(x_ref[...], g_ref[...]).astype(_BF16)

    o_ref[...] = jnp.dot(h_scr[...], w_ref[...], preferred_element_type=_F32).astype(o_ref.dtype)


def _proj(x2, g, w, out_dtype):
    t = x2.shape[0]
    n_cols = w.shape[1]
    tm = min(PROJ_TM, t)
    tn = min(PROJ_TN, n_cols)
    return pl.pallas_call(
        _proj_kernel,
        grid=(t // tm, n_cols // tn),
        in_specs=[pl.BlockSpec((tm, D_MODEL), lambda i, j: (i, 0)),
                  pl.BlockSpec((1, D_MODEL), lambda i, j: (0, 0)),
                  pl.BlockSpec((D_MODEL, tn), lambda i, j: (0, j))],
        out_specs=pl.BlockSpec((tm, tn), lambda i, j: (i, j)),
        out_shape=jax.ShapeDtypeStruct((t, n_cols), out_dtype),
        scratch_shapes=[pltpu.VMEM((tm, D_MODEL), _BF16)],
        compiler_params=pltpu.CompilerParams(dimension_semantics=("arbitrary", "arbitrary"),
                                             vmem_limit_bytes=VMEM_LIMIT),
        name="proj",
    )(x2, g, w)


def _conv_act(prev8, cur, next8, w, first, last):
    c = cur.shape[0]
    prev8 = jnp.where(first, 0.0, prev8[HALO_ROWS - CONV_HALO:].astype(_F32))
    next8 = jnp.where(last, 0.0, next8[:CONV_HALO].astype(_F32))
    ext = jnp.concatenate([prev8, cur.astype(_F32), next8], axis=0)
    rows = c + 2 * CONV_HALO
    y = None
    for j in range(CONV_K):
        sh = (CONV_K // 2 - j) % rows
        tap = ext if sh == 0 else pltpu.roll(ext, sh, 0)
        term = w[j:j + 1] * tap[CONV_HALO:CONV_HALO + c]
        y = term if y is None else y + term
    return _silu(y)


def _prep_kernel(dq, dp, dn, mq, mp, mn, dw_ref, mw_ref, dn_out, ml_out):
    n = pl.program_id(1)
    first = n == 0
    last = n == pl.num_programs(1) - 1
    x = _conv_act(dp[0], dq[0], dn[0], dw_ref[...], first, last)
    parts = []
    for h in range(2 * N_HEADS):
        t = _head(x, h)
        t = t * lax.rsqrt(jnp.sum(t * t, axis=-1, keepdims=True) + EPS)
        parts.append(t * (HEAD_DIM ** -0.5) if h < N_HEADS else t)
    parts.append(x[:, 2 * MIX_WIDTH:])
    dn_out[0] = jnp.concatenate(parts, axis=1).astype(_BF16)
    y = _conv_act(mp[0], mq[0], mn[0], mw_ref[...], first, last)
    ml_out[0] = jnp.concatenate([y[:, :MIX_WIDTH] * (HEAD_DIM ** -0.5), y[:, MIX_WIDTH:]], axis=1).astype(_BF16)


def _halo_specs(tm, seq, width, off):
    blk = off // width
    r8 = tm // HALO_ROWS
    last8 = seq // HALO_ROWS - 1
    return [pl.BlockSpec((1, tm, width), lambda b, n: (b, n, blk)),
            pl.BlockSpec((1, HALO_ROWS, width), lambda b, n: (b, jnp.maximum(n * r8 - 1, 0), blk)),
            pl.BlockSpec((1, HALO_ROWS, width), lambda b, n: (b, jnp.minimum((n + 1) * r8, last8), blk))]


def _prep_call(zb3, dn_conv, ml_conv):
    b, s, _ = zb3.shape
    tm = min(PREP_TM, s)
    in_specs = (_halo_specs(tm, s, DN_ACT_W, DN_QKV_OFF) + _halo_specs(tm, s, ML_ACT_W, ML_QK_OFF)
                + [_const_spec(dn_conv.shape), _const_spec(ml_conv.shape)])
    return pl.pallas_call(
        _prep_kernel,
        grid=(b, s // tm),
        in_specs=in_specs,
        out_specs=[pl.BlockSpec((1, tm, DN_ACT_W), lambda b, n: (b, n, 0)),
                   pl.BlockSpec((1, tm, ML_ACT_W), lambda b, n: (b, n, 0))],
        out_shape=[jax.ShapeDtypeStruct((b, s, DN_ACT_W), _BF16),
                   jax.ShapeDtypeStruct((b, s, ML_ACT_W), _BF16)],
        compiler_params=pltpu.CompilerParams(dimension_semantics=("arbitrary", "arbitrary"),
                                             vmem_limit_bytes=VMEM_LIMIT),
        name="prep",
    )(zb3, zb3, zb3, zb3, zb3, zb3, dn_conv, ml_conv)


def _pair_specs(c, n_chunks, width, off):
    blk = off // width
    return [pl.BlockSpec((1, c, width), lambda b, n: (b, n, blk)),
            pl.BlockSpec((1, c, width), lambda b, n: (b, n_chunks - 1 - n, blk))]


def _const_spec(shape):
    nd = len(shape)
    return pl.BlockSpec(shape, lambda b, n: (0,) * nd)


def _tri_masks(c, rev):
    row = lax.broadcasted_iota(jnp.int32, (c, c), 0)
    col = lax.broadcasted_iota(jnp.int32, (c, c), 1)
    return ((col >= row), (col > row)) if rev else ((col <= row), (col < row))


def _unit_tri_inverse_many(mats):
    c = mats[0].shape[0]
    row = lax.broadcasted_iota(jnp.int32, (c, c), 0)
    col = lax.broadcasted_iota(jnp.int32, (c, c), 1)
    blk = min(SOLVE_BASE, c)
    eye = jnp.where(row == col, 1.0, 0.0).astype(_F32)
    diag_blk = (row // blk) == (col // blk)
    a0 = [jnp.where(diag_blk, a, 0.0) for a in mats]
    ts = [eye - x for x in a0]
    pws = [_bdot(x, x) for x in a0]
    yield
    k = 2
    while 2 * k < blk:
        boths = [_bdot(jnp.concatenate([t, pw], axis=0), pw) for t, pw in zip(ts, pws)]
        ts = [t + both[:c] for t, both in zip(ts, boths)]
        pws = [both[c:] for both in boths]
        k *= 2
        yield
    ts = [t + _bdot(t, pw) for t, pw in zip(ts, pws)]
    yield
    while blk < c:
        off = ((row // (2 * blk)) == (col // (2 * blk))) & ((row // blk) != (col // blk))
        tl = [_bdot(t, jnp.where(off, a, 0.0)) for t, a in zip(ts, mats)]
        yield
        ts = [t - _bdot(x, t) for t, x in zip(ts, tl)]
        yield
        blk *= 2
    return ts


def _dn_body(af, smt_f, ab, smt_b, gp_ref, of_ref, ob_ref, s_ref):
    c = af.shape[1]
    n_g = 2 * N_HEADS
    gates = []
    for rev, smt in ((False, smt_f), (True, smt_b)):
        beta = _sigmoid(smt[SM_BETA:SM_BETA + n_g])
        g = -jnp.exp(gp_ref[n_g:2 * n_g]) * _softplus(smt[SM_ALPHA:SM_ALPHA + n_g] + gp_ref[0:n_g])
        gcum = _cumsum_lanes(g, rev)
        gtot = jnp.sum(g, axis=1, keepdims=True)
        stack = jnp.concatenate([beta, gcum, jnp.exp(gcum), jnp.exp(gtot - gcum),
                                 jnp.zeros((HEAD_DIM - 4 * n_g, c), _F32)], axis=0)
        gates.append(dict(cols=stack.T, gcum=gcum, e_tot=jnp.exp(gtot), masks=_tri_masks(c, rev)))
    yield
    acts = (af, ab)
    q, k, kb, vb, eg, a_low, attn = [], [], [], [], [], [], []
    for d, h in _CHAINS:
        g = gates[d]
        r = d * N_HEADS + h
        cols = g["cols"]
        incl, strict = g["masks"]
        qh = acts[d][0, :, h * HEAD_DIM:(h + 1) * HEAD_DIM]
        kh = acts[d][0, :, MIX_WIDTH + h * HEAD_DIM:MIX_WIDTH + (h + 1) * HEAD_DIM]
        vh = acts[d][0, :, 2 * MIX_WIDTH + h * HEAD_DIM:2 * MIX_WIDTH + (h + 1) * HEAD_DIM]
        beta = cols[:, r:r + 1]
        decay = jnp.exp(jnp.where(incl, cols[:, n_g + r:n_g + r + 1] - g["gcum"][r:r + 1, :], NEG_BIG))
        kbh = kh.astype(_F32) * beta
        both = _bdot_nt(jnp.concatenate([kbh.astype(_BF16), qh], axis=0), kh)
        a_low.append(jnp.where(strict, both[:c] * decay, 0.0))
        attn.append(both[c:] * decay)
        q.append(qh)
        k.append(kh)
        kb.append(kbh)
        vb.append(vh.astype(_F32) * beta)
        eg.append(cols[:, 2 * n_g + r:2 * n_g + r + 1])
        if h == N_HEADS - 1:
            yield
    tinv = yield from _unit_tri_inverse_many(a_low)
    sol = [_bdot(t, jnp.concatenate([vb[i], kb[i] * eg[i]], axis=1)) for i, t in enumerate(tinv)]
    yield
    ws_qs = [_bdot(jnp.concatenate([sol[i][:, HEAD_DIM:], q[i].astype(_F32) * eg[i]], axis=0), s_ref[d, h])
             for i, (d, h) in enumerate(_CHAINS)]
    yield
    v_new = [sol[i][:, :HEAD_DIM] - ws_qs[i][:c] for i in range(len(_CHAINS))]
    outs = [ws_qs[i][c:] + _bdot(attn[i], v_new[i]) for i in range(len(_CHAINS))]
    yield
    for i, (d, h) in enumerate(_CHAINS):
        r = d * N_HEADS + h
        er = gates[d]["cols"][:, 3 * n_g + r:3 * n_g + r + 1]
        et = gates[d]["e_tot"][r:r + 1, :]
        s_ref[d, h] = s_ref[d, h] * et + _bdot_tn(k[i].astype(_F32) * er, v_new[i])
    of_ref[0] = jnp.concatenate(outs[:N_HEADS], axis=1).astype(_BF16)
    ob_ref[0] = jnp.concatenate(outs[N_HEADS:], axis=1).astype(_BF16)


def _hg_level_masks(c, rev):
    row = lax.broadcasted_iota(jnp.int32, (c, c), 0)
    col = lax.broadcasted_iota(jnp.int32, (c, c), 1)
    levels = []
    blk = c // 2
    while blk >= SUB:
        same_pair = (row // (2 * blk)) == (col // (2 * blk))
        row_hi = (row // blk) % 2 == 1
        col_hi = (col // blk) % 2 == 1
        mask = (same_pair & (~row_hi) & col_hi) if rev else (same_pair & row_hi & (~col_hi))
        levels.append((blk, mask))
        blk //= 2
    return levels


def _hg_band_masks(c, rev):
    row = lax.broadcasted_iota(jnp.int32, (c, c), 0)
    col = lax.broadcasted_iota(jnp.int32, (c, c), 1)
    masks = []
    for dl in range(SUB):
        if rev:
            masks.append((col == row + dl) & ((row % SUB) + dl < SUB))
        else:
            masks.append((col == row - dl) & ((row % SUB) >= dl))
    return masks


def _group_roll(x, shift):
    c, w = x.shape
    return pltpu.roll(x.reshape(c // 8, 8, w), shift % 8, 1).reshape(c, w)


def _pair_reference(b, blk, rev):
    c, w = b.shape
    b3 = b.reshape(c // (2 * blk), 2 * blk, w)
    ref = b3[:, blk:blk + 1, :] if rev else b3[:, blk - 1:blk, :]
    return jnp.broadcast_to(ref, b3.shape).reshape(c, w)


def _hg_body(qf, ff, vf, qb, fb, vb, lb_ref, of_ref, ob_ref, s_ref):
    c = qf.shape[1]
    dirs = []
    for d, (q_ref, f_ref, v_ref) in enumerate(((qf, ff, vf), (qb, fb, vb))):
        rev = d == 1
        lb = lb_ref[d:d + 1, :]
        sig = _sigmoid(f_ref[0])
        q_all = _silu(q_ref[0].astype(_F32))
        k_all = (1.0 - lb) * (1.0 - sig)
        f_all = lb + (1.0 - lb) * sig
        logf = jnp.log(f_all)
        b_all = _cumsum_rows(logf, rev)
        btot = jnp.sum(logf, axis=0, keepdims=True)
        dirs.append(dict(rev=rev, q=q_all, k=k_all, f=f_all, v=v_ref[0],
                         b2=b_all * LOG2_E,
                         qe=q_all * jnp.exp(b_all), ke=k_all * jnp.exp(btot - b_all), e_tot=jnp.exp(btot),
                         levels=_hg_level_masks(c, rev), bands=_hg_band_masks(c, rev)))
        yield
    lvl_ops, band_sums = [], []
    for d, h in _CHAINS:
        g = dirs[d]
        rev = g["rev"]
        q = _head(g["q"], h)
        k = _head(g["k"], h)
        f = _head(g["f"], h)
        b2 = _head(g["b2"], h)
        q16 = q.astype(_BF16)
        k16 = k.astype(_BF16)
        ops = []
        for blk, _ in g["levels"]:
            e = jnp.exp2(-jnp.abs(b2 - _pair_reference(b2, blk, rev))).astype(_BF16)
            ops.append((q16 * e, k16 * e))
        lvl_ops.append(ops)
        sums = []
        qd = q
        for dl in range(SUB):
            if dl > 0:
                qd = qd * (f if dl == 1 else _group_roll(f, (1 - dl) if rev else (dl - 1)))
            k_dl = k if dl == 0 else _group_roll(k, -dl if rev else dl)
            sums.append(jnp.sum(qd * k_dl, axis=-1, keepdims=True))
        band_sums.append(sums)
        if h % 2 == 1:
            yield
    prods = []
    for i in range(len(_CHAINS)):
        prods.append([_bdot_nt(qs, ks) for qs, ks in lvl_ops[i]])
        if i % 2 == 1:
            yield
    attn = []
    for i, (d, h) in enumerate(_CHAINS):
        g = dirs[d]
        a = jnp.zeros((c, c), _F32)
        for (_, mask), prod in zip(g["levels"], prods[i]):
            a = jnp.where(mask, prod, a)
        for dl in range(SUB):
            a = jnp.where(g["bands"][dl], band_sums[i][dl], a)
        attn.append(a)
    yield
    outs = []
    for i, (d, h) in enumerate(_CHAINS):
        g = dirs[d]
        state_t = s_ref[d, h]
        outs.append(_bdot_nt(_head(g["qe"], h), state_t) + _bdot(attn[i], _head(g["v"], h)))
        s_ref[d, h] = state_t * _head(g["e_tot"], h) + _bdot_tn(_head(g["v"], h), _head(g["ke"], h))
        if h % 2 == 1:
            yield
    of_ref[0] = jnp.concatenate(outs[:N_HEADS], axis=1).astype(_BF16)
    ob_ref[0] = jnp.concatenate(outs[N_HEADS:], axis=1).astype(_BF16)


def _ml_body(af, vf, smt_f, ab, vb, smt_b, gp_ref, of_ref, ob_ref, c_ref, m_ref):
    c = af.shape[1]
    n_g = 2 * N_HEADS
    lane = lax.broadcasted_iota(jnp.int32, (c, HEAD_DIM), 1)
    ones_col = jnp.where(lane == 0, 1.0, 0.0).astype(_BF16)
    gates = []
    for rev, smt in ((False, smt_f), (True, smt_b)):
        ig = smt[SM_IG:SM_IG + n_g] + gp_ref[0:n_g]
        f_t = smt[SM_FG:SM_FG + n_g] + gp_ref[n_g:2 * n_g]
        logf = jnp.minimum(f_t, 0.0) - jnp.log(1.0 + jnp.exp(-jnp.abs(f_t)))
        bcum = _cumsum_lanes(logf, rev)
        cols = jnp.concatenate([ig, bcum, jnp.zeros((HEAD_DIM - 2 * n_g, c), _F32)], axis=0).T
        gates.append(dict(ig=ig, bcum=bcum, btot=jnp.sum(logf, axis=1, keepdims=True), cols=cols,
                          incl=_tri_masks(c, rev)[0]))
    acts = (af, ab)
    vals = (vf, vb)
    q, k, v_ext, qk, qs = [], [], [], [], []
    for d, h in _CHAINS:
        qh = acts[d][0, :, h * HEAD_DIM:(h + 1) * HEAD_DIM]
        kh = acts[d][0, :, MIX_WIDTH + h * HEAD_DIM:MIX_WIDTH + (h + 1) * HEAD_DIM]
        vh = vals[d][0, :, h * HEAD_DIM:(h + 1) * HEAD_DIM]
        q.append(qh)
        k.append(kh)
        v_ext.append(jnp.concatenate([vh.astype(_BF16), ones_col], axis=1))
        qk.append(_bdot_nt(qh, kh))
        qs.append(_bdot(qh, c_ref[d, h]))
        if h == N_HEADS - 1:
            yield
    n_ch = len(_CHAINS)
    logw, inter, logw_end, bt_m = [], [], [], []
    for d, h in _CHAINS:
        g = gates[d]
        col = d * N_HEADS + h
        b_col = g["cols"][:, n_g + col:n_g + col + 1]
        i_col = g["cols"][:, col:col + 1]
        b_row = g["bcum"][col:col + 1, :]
        i_row = g["ig"][col:col + 1, :]
        bt = g["btot"][col:col + 1, :]
        m_old = m_ref[col:col + 1, 0:1]
        logw.append(jnp.where(g["incl"], b_col - b_row + i_row, NEG_BIG))
        inter.append(b_col + m_old)
        logw_end.append(bt - b_col + i_col)
        bt_m.append(bt + m_old)
    yield
    m_row = [jnp.maximum(inter[i], jnp.max(logw[i], axis=-1, keepdims=True)) for i in range(n_ch)]
    m_new = [jnp.maximum(bt_m[i], jnp.max(logw_end[i], axis=0, keepdims=True)) for i in range(n_ch)]
    yield
    p = [qk[i] * jnp.exp(logw[i] - m_row[i]) for i in range(n_ch)]
    yield
    pv = [_bdot(p[i], v_ext[i]) for i in range(n_ch)]
    yield
    outs = []
    for i in range(n_ch):
        nd = jnp.exp(inter[i] - m_row[i]) * qs[i] + pv[i]
        den = nd[:, HEAD_DIM:HEAD_DIM + 1]
        outs.append(nd[:, :HEAD_DIM] / jnp.maximum(jnp.abs(den), jnp.exp(-m_row[i])))
    yield
    kw = [k[i].astype(_F32) * jnp.exp(logw_end[i] - m_new[i]) for i in range(n_ch)]
    for i, (d, h) in enumerate(_CHAINS):
        col = d * N_HEADS + h
        c_ref[d, h] = jnp.exp(bt_m[i] - m_new[i]) * c_ref[d, h] + _bdot_tn(kw[i], v_ext[i])
        m_ref[col:col + 1, :] = jnp.broadcast_to(m_new[i], (1, HEAD_DIM))
    of_ref[0] = jnp.concatenate(outs[:N_HEADS], axis=1).astype(_BF16)
    ob_ref[0] = jnp.concatenate(outs[N_HEADS:], axis=1).astype(_BF16)


def _mix_kernel(sf, sb, dn_af, dn_ab, dn_gp, hg_qf, hg_ff, hg_vf, hg_qb, hg_fb, hg_vb, hg_lb,
                ml_af, ml_vf, ml_ab, ml_vb, ml_gp,
                dn_of, dn_ob, hg_of, hg_ob, ml_of, ml_ob, dn_s, hg_s, ml_c, ml_m):
    @pl.when(pl.program_id(1) == 0)
    def _():
        dn_s[...] = jnp.zeros_like(dn_s)
        hg_s[...] = jnp.zeros_like(hg_s)
        ml_c[...] = jnp.zeros_like(ml_c)
        ml_m[...] = jnp.full(ml_m.shape, NEG_BIG, _F32)

    smt_f = sf[0].T
    smt_b = sb[0].T
    dn = _dn_body(dn_af, smt_f, dn_ab, smt_b, dn_gp, dn_of, dn_ob, dn_s)
    hg = _hg_body(hg_qf, hg_ff, hg_vf, hg_qb, hg_fb, hg_vb, hg_lb, hg_of, hg_ob, hg_s)
    ml = _ml_body(ml_af, ml_vf, smt_f, ml_ab, ml_vb, smt_b, ml_gp, ml_of, ml_ob, ml_c, ml_m)
    pending = [dn, hg, dn, ml]
    while pending:
        for gen in list(pending):
            if gen in pending and next(gen, _DONE) is _DONE:
                pending = [g for g in pending if g is not gen]


def _mix_call(zb3, zf3, dn_act, ml_act, dn_gates, hg_lb, ml_gates):
    b, s, _ = zb3.shape
    c = min(CHUNK, s)
    n_chunks = s // c
    pair = lambda width, off: _pair_specs(c, n_chunks, width, off)
    sm, dn_a, ml_a = pair(HEAD_DIM, SM_OFF), pair(DN_ACT_W, 0), pair(ML_ACT_W, 0)
    hg_q, hg_v, ml_v = pair(MIX_WIDTH, HG_Q_OFF), pair(MIX_WIDTH, HG_I_OFF), pair(MIX_WIDTH, ML_V_OFF)
    hg_ff = pair(MIX_WIDTH, HG_F_OFF)[0]
    hg_fb = pair(MIX_WIDTH, HG_F_OFF + MIX_WIDTH)[1]
    in_specs = [sm[0], sm[1], dn_a[0], dn_a[1], _const_spec(dn_gates.shape),
                hg_q[0], hg_ff, hg_v[0], hg_q[1], hg_fb, hg_v[1], _const_spec(hg_lb.shape),
                ml_a[0], ml_v[0], ml_a[1], ml_v[1], _const_spec(ml_gates.shape)]
    args = (zf3, zf3, dn_act, dn_act, dn_gates, zb3, zf3, zb3, zb3, zf3, zb3, hg_lb,
            ml_act, zb3, ml_act, zb3, ml_gates)
    out = jax.ShapeDtypeStruct((b, s, MIX_WIDTH), _BF16)
    state = (2, N_HEADS, HEAD_DIM, HEAD_DIM)
    return pl.pallas_call(
        _mix_kernel,
        grid=(b, n_chunks),
        in_specs=in_specs,
        out_specs=pair(MIX_WIDTH, 0) * 3,
        out_shape=[out] * 6,
        scratch_shapes=[pltpu.VMEM(state, _F32), pltpu.VMEM(state, _F32),
                        pltpu.VMEM((2, N_HEADS, HEAD_DIM, 2 * HEAD_DIM), _F32),
                        pltpu.VMEM((2 * N_HEADS, HEAD_DIM), _F32)],
        compiler_params=pltpu.CompilerParams(dimension_semantics=("arbitrary", "arbitrary"),
                                             vmem_limit_bytes=VMEM_LIMIT),
        name="mixers",
    )(*args)


def _head_norm(o, w):
    parts = []
    for h in range(N_HEADS):
        seg = _head(o, h)
        parts.append(seg * lax.rsqrt(jnp.mean(seg * seg, axis=-1, keepdims=True) + EPS))
    return jnp.concatenate(parts, axis=1) * w


def _merge_kernel(x_ref, dnf, dnb, hgf, hgb, mlf, mlb, dnz, hgz, mlo, gp_ref,
                  nw_ref, wbr_ref, wout_ref, gpost_ref, o_ref):
    both = lambda f, b: f[...].astype(_F32) + b[...].astype(_F32)
    branches = (
        _head_norm(both(dnf, dnb), nw_ref[0:1, :]) * _silu(dnz[...].astype(_F32)),
        _head_norm(both(hgf, hgb), nw_ref[1:2, :]) * _silu(hgz[...].astype(_F32)),
        _head_norm(both(mlf, mlb), nw_ref[2:3, :]) * _sigmoid(mlo[...].astype(_F32)),
    )
    merged = None
    for i, br in enumerate(branches):
        gate = _sigmoid(gp_ref[:, i * D_MODEL:(i + 1) * D_MODEL].astype(_F32))
        term = gate * _bdot(br, wbr_ref[i])
        merged = term if merged is None else merged + term
    y = _bdot(merged, wout_ref[...])
    o_ref[...] = x_ref[...] + _rmsnorm(y, gpost_ref[...])


def _merge_call(x2, zb2, mix_outs, norm_w, w_branch, w_out, g_post):
    t = x2.shape[0]
    tm = min(MERGE_TM, t)
    row = lambda w: pl.BlockSpec((tm, w), lambda i: (i, 0))
    zblk = lambda w, off: pl.BlockSpec((tm, w), lambda i: (i, off // w))
    full = lambda a: pl.BlockSpec(a.shape, lambda i: (0,) * a.ndim)
    in_specs = ([row(D_MODEL)] + [row(MIX_WIDTH)] * 6
                + [zblk(MIX_WIDTH, DN_Z_OFF), zblk(MIX_WIDTH, HG_Z_OFF), zblk(MIX_WIDTH, ML_O_OFF),
                   zblk(N_BRANCH * D_MODEL, GP_OFF)]
                + [full(norm_w), full(w_branch), full(w_out), full(g_post)])
    return pl.pallas_call(
        _merge_kernel,
        grid=(t // tm,),
        in_specs=in_specs,
        out_specs=row(D_MODEL),
        out_shape=jax.ShapeDtypeStruct((t, D_MODEL), _F32),
        compiler_params=pltpu.CompilerParams(dimension_semantics=("arbitrary",),
                                             vmem_limit_bytes=VMEM_LIMIT),
        name="merge",
    )(x2, *mix_outs, zb2, zb2, zb2, zb2, norm_w, w_branch, w_out, g_post)


def _ffn_kernel(x_ref, gpre_ref, w1_ref, w2_ref, gpost_ref, o_ref):
    x = x_ref[...]
    u = jnp.maximum(_bdot(_rmsnorm(x, gpre_ref[...]), w1_ref[...]), 0.0)
    y = _bdot(u * u, w2_ref[...])
    o_ref[...] = x + _rmsnorm(y, gpost_ref[...])


def _ffn_call(x2, g_pre, w1, w2, g_post):
    t = x2.shape[0]
    tm = min(FFN_TM, t)
    row = pl.BlockSpec((tm, D_MODEL), lambda i: (i, 0))
    full = lambda a: pl.BlockSpec(a.shape, lambda i: (0,) * a.ndim, pipeline_mode=pl.Buffered(1))
    return pl.pallas_call(
        _ffn_kernel,
        grid=(t // tm,),
        in_specs=[row, full(g_pre), full(w1), full(w2), full(g_post)],
        out_specs=row,
        out_shape=jax.ShapeDtypeStruct((t, D_MODEL), _F32),
        compiler_params=pltpu.CompilerParams(dimension_semantics=("arbitrary",),
                                             vmem_limit_bytes=VMEM_LIMIT),
        name="ffn",
    )(x2, g_pre, w1, w2, g_post)


def _permute_w_in(w_in):
    m = MIX_WIDTH
    o_dn, o_sm_dn, o_hg, o_ml, o_sm_ml, o_gp = 0, 4 * m, 4 * m + 16, 9 * m + 16, 13 * m + 16, 13 * m + 32
    wide = [
        w_in[:, o_gp:o_gp + N_BRANCH * D_MODEL],
        w_in[:, o_dn:o_dn + 4 * m],
        w_in[:, o_ml:o_ml + 4 * m],
        w_in[:, o_hg:o_hg + m],
        w_in[:, o_hg + 3 * m:o_hg + 5 * m],
        jnp.zeros((w_in.shape[0], NB_COLS - HG_Z_OFF - m), w_in.dtype),
    ]
    gates = [
        w_in[:, o_hg + m:o_hg + 3 * m],
        w_in[:, o_sm_dn:o_sm_dn + 16],
        w_in[:, o_sm_ml:o_sm_ml + 16],
        jnp.zeros((w_in.shape[0], NF_COLS - SM_OFF - 32), w_in.dtype),
    ]
    return jnp.concatenate(wide, axis=1).astype(_BF16), jnp.concatenate(gates, axis=1).astype(_BF16)


def _gate_rows(first, second):
    vals = jnp.concatenate([first.astype(_F32).reshape(-1), second.astype(_F32).reshape(-1)])
    return jnp.broadcast_to(vals[:, None], (vals.shape[0], HEAD_DIM))


def _layer(x3, p):
    b, s, _ = x3.shape
    x2 = x3.reshape(b * s, D_MODEL)
    w_wide, w_gates = p["w_in"]
    zb2 = _proj(x2, p["norm_mix_pre"], w_wide, _BF16)
    zb3 = zb2.reshape(b, s, NB_COLS)
    zf3 = _proj(x2, p["norm_mix_pre"], w_gates, _F32).reshape(b, s, NF_COLS)
    dn_act, ml_act = _prep_call(zb3, p["dn_conv"], p["ml_conv"])
    mix = _mix_call(zb3, zf3, dn_act, ml_act, p["dn_gates"], p["hg_lb"], p["ml_gates"])
    mix = [a.reshape(b * s, MIX_WIDTH) for a in mix]
    x2 = _merge_call(x2, zb2, mix, p["mix_norm"], p["w_branch"], p["w_out"], p["norm_mix_post"])
    x2 = _ffn_call(x2, p["norm_ffn_pre"], p["w_ff1"], p["w_ff2"], p["norm_ffn_post"])
    return x2.reshape(b, s, D_MODEL)


def kernel(x_prompt, x_sample, norm_mix_pre, norm_mix_post, norm_ffn_pre, norm_ffn_post, w_in, dn_conv,
           dn_a_log, dn_dt_bias, dn_norm, hg_lb, hg_norm, ml_conv, ml_i_bias, ml_f_bias, ml_norm,
           w_branch, w_out, w_ff1, w_ff2):
    depth = w_in.shape[0]
    p_layers = jax.nn.softmax(hg_lb.astype(_F32), axis=0)
    lower_bounds = jnp.cumsum(p_layers, axis=0) - p_layers[0]
    y_prompt, y_sample = x_prompt, x_sample
    for l in range(depth):
        p = {
            "norm_mix_pre": norm_mix_pre[l].reshape(1, D_MODEL),
            "norm_mix_post": norm_mix_post[l].reshape(1, D_MODEL),
            "norm_ffn_pre": norm_ffn_pre[l].reshape(1, D_MODEL),
            "norm_ffn_post": norm_ffn_post[l].reshape(1, D_MODEL),
            "w_in": _permute_w_in(w_in[l]),
            "dn_conv": dn_conv[l],
            "dn_gates": _gate_rows(dn_dt_bias[l], dn_a_log[l]),
            "hg_lb": lower_bounds[l],
            "ml_conv": ml_conv[l],
            "ml_gates": _gate_rows(ml_i_bias[l], ml_f_bias[l]),
            "mix_norm": jnp.stack([dn_norm[l].reshape(-1), hg_norm[l].reshape(-1), ml_norm[l].reshape(-1)]
                                  + [jnp.zeros((MIX_WIDTH,), _F32)] * 5),
            "w_branch": w_branch[l].astype(_BF16),
            "w_out": w_out[l].astype(_BF16),
            "w_ff1": w_ff1[l].astype(_BF16),
            "w_ff2": w_ff2[l].astype(_BF16),
        }
        y_prompt = _layer(y_prompt, p)
        y_sample = _layer(y_sample, p)
    return (y_prompt, y_sample)
```

```python
import jax
import jax.numpy as jnp
from jax import lax
from jax.experimental import pallas as pl
from jax.experimental.pallas import tpu as pltpu

D_MODEL = 1024
N_HEADS = 4
HEAD_DIM = 128
MIX_WIDTH = N_HEADS * HEAD_DIM
N_BRANCH = 3
CONV_K = 5
CONV_HALO = 8
HALO_ROWS = 16
D_FF = 4 * D_MODEL
EPS = 1e-6
LOG2_E = 1.4426950408889634
NEG_BIG = -1e30
CHUNK = 128
MIX_ROWS = 4 * CHUNK
SUB = 4
SOLVE_BASE = 16

GP_OFF = 0
DN_QKV_OFF = 3072
DN_Z_OFF = 4608
ML_QK_OFF = 5120
ML_V_OFF = 6144
ML_O_OFF = 6656
HG_Q_OFF = 7168
HG_I_OFF = 7680
HG_Z_OFF = 8192
NB_COLS = 8960
HG_F_OFF = 0
SM_OFF = 1024
NF_COLS = 1152
SM_BETA, SM_ALPHA, SM_IG, SM_FG = 0, 8, 16, 24
DN_ACT_W = 3 * MIX_WIDTH
ML_ACT_W = 2 * MIX_WIDTH

PROJ_TM = 1024
PROJ_TN = 1792
PREP_TM = 256
MERGE_TM = 512
FFN_TM = 512
VMEM_LIMIT = 56 * 1024 * 1024

_F32 = jnp.float32
_BF16 = jnp.bfloat16
_CHAINS = tuple((d, h) for d in range(2) for h in range(N_HEADS))
_DONE = object()


def _bdot(a, b):
    return jnp.dot(a.astype(_BF16), b.astype(_BF16), preferred_element_type=_F32)


def _bdot_nt(a, b):
    return lax.dot_general(a.astype(_BF16), b.astype(_BF16), (((1,), (1,)), ((), ())),
                           preferred_element_type=_F32)


def _bdot_tn(a, b):
    return lax.dot_general(a.astype(_BF16), b.astype(_BF16), (((0,), (0,)), ((), ())),
                           preferred_element_type=_F32)


def _sigmoid(x):
    return 0.5 * jnp.tanh(0.5 * x) + 0.5


def _silu_of_half(h):
    return h * (jnp.tanh(h) + 1.0)


def _two_sigmoid_of_half(h):
    return jnp.tanh(h) + 1.0


def _softplus(x):
    return jnp.maximum(x, 0.0) + jnp.log(1.0 + jnp.exp(-jnp.abs(x)))


def _rmsnorm(x, g):
    return x * lax.rsqrt(jnp.mean(x * x, axis=-1, keepdims=True) + EPS) * g


def _split3(x):
    mask = jnp.uint32(0xFFFF0000)

    def top(v):
        return lax.bitcast_convert_type(lax.bitcast_convert_type(v, jnp.uint32) & mask, _F32)

    hi = top(x)
    r1 = x - hi
    mid = top(r1)
    return hi.astype(_BF16), mid.astype(_BF16), (r1 - mid).astype(_BF16)


def _cumsum_rows(x, rev):
    c, w = x.shape
    row = lax.broadcasted_iota(jnp.int32, (c, c), 0)
    col = lax.broadcasted_iota(jnp.int32, (c, c), 1)
    tri = jnp.where((col >= row) if rev else (col <= row), 1.0, 0.0).astype(_BF16)
    s = jnp.dot(tri, jnp.concatenate(_split3(x), axis=1), preferred_element_type=_F32)
    return s[:, :w] + s[:, w:2 * w] + s[:, 2 * w:]


def _cumsum_lanes(x, rev):
    r, c = x.shape
    row = lax.broadcasted_iota(jnp.int32, (c, c), 0)
    col = lax.broadcasted_iota(jnp.int32, (c, c), 1)
    tri = jnp.where((row >= col) if rev else (row <= col), 1.0, 0.0).astype(_BF16)
    s = jnp.dot(jnp.concatenate(_split3(x), axis=0), tri, preferred_element_type=_F32)
    return s[:r] + s[r:2 * r] + s[2 * r:]


def _head(t, h):
    return t[:, h * HEAD_DIM:(h + 1) * HEAD_DIM]


def _proj_kernel(x_ref, g_ref, w_ref, o_ref, h_scr):
    @pl.when(pl.program_id(1) == 0)
    def _():
        h_scr[...] = _rmsnorm(x_ref[...], g_ref[...]).astype(_BF16)

    o_ref[...] = jnp.dot(h_scr[...], w_ref[...], preferred_element_type=_F32).astype(o_ref.dtype)


def _proj(x2, g, w, out_dtype):
    t = x2.shape[0]
    n_cols = w.shape[1]
    tm = min(PROJ_TM, t)
    tn = min(PROJ_TN, n_cols)
    return pl.pallas_call(
        _proj_kernel,
        grid=(t // tm, n_cols // tn),
        in_specs=[pl.BlockSpec((tm, D_MODEL), lambda i, j: (i, 0)),
                  pl.BlockSpec((1, D_MODEL), lambda i, j: (0, 0)),
                  pl.BlockSpec((D_MODEL, tn), lambda i, j: (0, j))],
        out_specs=pl.BlockSpec((tm, tn), lambda i, j: (i, j)),
        out_shape=jax.ShapeDtypeStruct((t, n_cols), out_dtype),
        scratch_shapes=[pltpu.VMEM((tm, D_MODEL), _BF16)],
        compiler_params=pltpu.CompilerParams(dimension_semantics=("arbitrary", "arbitrary"),
                                             vmem_limit_bytes=VMEM_LIMIT),
        name="proj",
    )(x2, g, w)


def _conv_act(prev8, cur, next8, w, first, last):
    c = cur.shape[0]
    prev8 = jnp.where(first, 0.0, prev8[HALO_ROWS - CONV_HALO:].astype(_F32))
    next8 = jnp.where(last, 0.0, next8[:CONV_HALO].astype(_F32))
    ext = jnp.concatenate([prev8, cur.astype(_F32), next8], axis=0)
    rows = c + 2 * CONV_HALO
    y = None
    for j in range(CONV_K):
        sh = (CONV_K // 2 - j) % rows
        tap = ext if sh == 0 else pltpu.roll(ext, sh, 0)
        term = w[j:j + 1] * tap[CONV_HALO:CONV_HALO + c]
        y = term if y is None else y + term
    return _silu_of_half(y)


def _prep_kernel(dq, dp, dn, mq, mp, mn, dw_ref, mw_ref, dn_out, ml_out):
    n = pl.program_id(1)
    first = n == 0
    last = n == pl.num_programs(1) - 1
    for h in range(3 * N_HEADS):
        sl = slice(h * HEAD_DIM, (h + 1) * HEAD_DIM)
        t = _conv_act(dp[0, :, sl], dq[0, :, sl], dn[0, :, sl], dw_ref[:, sl], first, last)
        if h < 2 * N_HEADS:
            inv = lax.rsqrt(jnp.sum(t * t, axis=-1, keepdims=True) + EPS)
            t = t * (inv * (HEAD_DIM ** -0.5) if h < N_HEADS else inv)
        dn_out[0, :, sl] = t.astype(_BF16)
    for h in range(2 * N_HEADS):
        sl = slice(h * HEAD_DIM, (h + 1) * HEAD_DIM)
        t = _conv_act(mp[0, :, sl], mq[0, :, sl], mn[0, :, sl], mw_ref[:, sl], first, last)
        ml_out[0, :, sl] = (t * (HEAD_DIM ** -0.5) if h < N_HEADS else t).astype(_BF16)


def _halo_specs(tm, seq, width, off):
    blk = off // width
    r8 = tm // HALO_ROWS
    last8 = seq // HALO_ROWS - 1
    return [pl.BlockSpec((1, tm, width), lambda b, n: (b, n, blk)),
            pl.BlockSpec((1, HALO_ROWS, width), lambda b, n: (b, jnp.maximum(n * r8 - 1, 0), blk)),
            pl.BlockSpec((1, HALO_ROWS, width), lambda b, n: (b, jnp.minimum((n + 1) * r8, last8), blk))]


def _prep_call(zb3, dn_conv, ml_conv):
    b, s, _ = zb3.shape
    tm = min(PREP_TM, s)
    in_specs = (_halo_specs(tm, s, DN_ACT_W, DN_QKV_OFF) + _halo_specs(tm, s, ML_ACT_W, ML_QK_OFF)
                + [_const_spec(dn_conv.shape), _const_spec(ml_conv.shape)])
    return pl.pallas_call(
        _prep_kernel,
        grid=(b, s // tm),
        in_specs=in_specs,
        out_specs=[pl.BlockSpec((1, tm, DN_ACT_W), lambda b, n: (b, n, 0)),
                   pl.BlockSpec((1, tm, ML_ACT_W), lambda b, n: (b, n, 0))],
        out_shape=[jax.ShapeDtypeStruct((b, s, DN_ACT_W), _BF16),
                   jax.ShapeDtypeStruct((b, s, ML_ACT_W), _BF16)],
        compiler_params=pltpu.CompilerParams(dimension_semantics=("arbitrary", "arbitrary"),
                                             vmem_limit_bytes=VMEM_LIMIT),
        name="prep",
    )(zb3, zb3, zb3, zb3, zb3, zb3, dn_conv, ml_conv)


def _pair_specs(c, n_chunks, width, off):
    blk = off // width
    return [pl.BlockSpec((1, c, width), lambda b, n: (b, n, blk)),
            pl.BlockSpec((1, c, width), lambda b, n: (b, n_chunks - 1 - n, blk))]


def _const_spec(shape):
    nd = len(shape)
    return pl.BlockSpec(shape, lambda b, n: (0,) * nd)


def _tri_masks(c, rev):
    row = lax.broadcasted_iota(jnp.int32, (c, c), 0)
    col = lax.broadcasted_iota(jnp.int32, (c, c), 1)
    return ((col >= row), (col > row)) if rev else ((col <= row), (col < row))


def _unit_tri_inverse_many(mats):
    c = mats[0].shape[0]
    row = lax.broadcasted_iota(jnp.int32, (c, c), 0)
    col = lax.broadcasted_iota(jnp.int32, (c, c), 1)
    blk = min(SOLVE_BASE, c)
    eye = jnp.where(row == col, 1.0, 0.0).astype(_F32)
    diag_blk = (row // blk) == (col // blk)
    a0 = [jnp.where(diag_blk, a, 0.0) for a in mats]
    ts = [eye - x for x in a0]
    pws = [_bdot(x, x) for x in a0]
    yield
    k = 2
    while 2 * k < blk:
        boths = [_bdot(jnp.concatenate([t, pw], axis=0), pw) for t, pw in zip(ts, pws)]
        ts = [t + both[:c] for t, both in zip(ts, boths)]
        pws = [both[c:] for both in boths]
        k *= 2
        yield
    ts = [t + _bdot(t, pw) for t, pw in zip(ts, pws)]
    yield
    while blk < c:
        off = ((row // (2 * blk)) == (col // (2 * blk))) & ((row // blk) != (col // blk))
        tl = [_bdot(t, jnp.where(off, a, 0.0)) for t, a in zip(ts, mats)]
        yield
        ts = [t - _bdot(x, t) for t, x in zip(ts, tl)]
        yield
        blk *= 2
    return ts


def _dn_body(rows, af, smt_f, ab, smt_b, gp_ref, of_ref, ob_ref, s_ref):
    c = CHUNK
    n_g = 2 * N_HEADS
    gates = []
    for rev, smt in ((False, smt_f), (True, smt_b)):
        beta = _sigmoid(smt[SM_BETA:SM_BETA + n_g])
        g = -jnp.exp(gp_ref[n_g:2 * n_g]) * _softplus(smt[SM_ALPHA:SM_ALPHA + n_g] + gp_ref[0:n_g])
        gcum = _cumsum_lanes(g, rev)
        gtot = jnp.sum(g, axis=1, keepdims=True)
        stack = jnp.concatenate([beta, gcum, jnp.exp(gcum), jnp.exp(gtot - gcum),
                                 jnp.zeros((HEAD_DIM - 4 * n_g, c), _F32)], axis=0)
        gates.append(dict(cols=stack.T, gcum=gcum, e_tot=jnp.exp(gtot), masks=_tri_masks(c, rev)))
    yield
    acts = (af, ab)
    q, k, kb, vb, eg, a_low, attn = [], [], [], [], [], [], []
    for d, h in _CHAINS:
        g = gates[d]
        r = d * N_HEADS + h
        cols = g["cols"]
        incl, strict = g["masks"]
        qh = acts[d][0, rows[d], h * HEAD_DIM:(h + 1) * HEAD_DIM]
        kh = acts[d][0, rows[d], MIX_WIDTH + h * HEAD_DIM:MIX_WIDTH + (h + 1) * HEAD_DIM]
        vh = acts[d][0, rows[d], 2 * MIX_WIDTH + h * HEAD_DIM:2 * MIX_WIDTH + (h + 1) * HEAD_DIM]
        beta = cols[:, r:r + 1]
        decay = jnp.exp(jnp.where(incl, cols[:, n_g + r:n_g + r + 1] - g["gcum"][r:r + 1, :], NEG_BIG))
        kbh = kh.astype(_F32) * beta
        both = _bdot_nt(jnp.concatenate([kbh.astype(_BF16), qh], axis=0), kh)
        a_low.append(jnp.where(strict, both[:c] * decay, 0.0))
        attn.append(both[c:] * decay)
        q.append(qh)
        k.append(kh)
        kb.append(kbh)
        vb.append(vh.astype(_F32) * beta)
        eg.append(cols[:, 2 * n_g + r:2 * n_g + r + 1])
        if h == N_HEADS - 1:
            yield
    tinv = yield from _unit_tri_inverse_many(a_low)
    sol = [_bdot(t, jnp.concatenate([vb[i], kb[i] * eg[i]], axis=1)) for i, t in enumerate(tinv)]
    yield
    ws_qs = [_bdot(jnp.concatenate([sol[i][:, HEAD_DIM:], q[i].astype(_F32) * eg[i]], axis=0), s_ref[d, h])
             for i, (d, h) in enumerate(_CHAINS)]
    yield
    v_new = [sol[i][:, :HEAD_DIM] - ws_qs[i][:c] for i in range(len(_CHAINS))]
    outs = [ws_qs[i][c:] + _bdot(attn[i], v_new[i]) for i in range(len(_CHAINS))]
    yield
    for i, (d, h) in enumerate(_CHAINS):
        r = d * N_HEADS + h
        er = gates[d]["cols"][:, 3 * n_g + r:3 * n_g + r + 1]
        et = gates[d]["e_tot"][r:r + 1, :]
        s_ref[d, h] = s_ref[d, h] * et + _bdot_tn(k[i].astype(_F32) * er, v_new[i])
    of_ref[0, rows[0], :] = jnp.concatenate(outs[:N_HEADS], axis=1).astype(_BF16)
    ob_ref[0, rows[1], :] = jnp.concatenate(outs[N_HEADS:], axis=1).astype(_BF16)


def _hg_level_masks(c, rev):
    row = lax.broadcasted_iota(jnp.int32, (c, c), 0)
    col = lax.broadcasted_iota(jnp.int32, (c, c), 1)
    levels = []
    blk = c // 2
    while blk >= SUB:
        same_pair = (row // (2 * blk)) == (col // (2 * blk))
        row_hi = (row // blk) % 2 == 1
        col_hi = (col // blk) % 2 == 1
        mask = (same_pair & (~row_hi) & col_hi) if rev else (same_pair & row_hi & (~col_hi))
        levels.append((blk, mask))
        blk //= 2
    return levels


def _hg_band_masks(c, rev):
    row = lax.broadcasted_iota(jnp.int32, (c, c), 0)
    col = lax.broadcasted_iota(jnp.int32, (c, c), 1)
    masks = []
    for dl in range(SUB):
        if rev:
            masks.append((col == row + dl) & ((row % SUB) + dl < SUB))
        else:
            masks.append((col == row - dl) & ((row % SUB) >= dl))
    return masks


def _group_roll(x, shift):
    c, w = x.shape
    return pltpu.roll(x.reshape(c // 8, 8, w), shift % 8, 1).reshape(c, w)


def _pair_reference(b, blk, rev):
    c, w = b.shape
    b3 = b.reshape(c // (2 * blk), 2 * blk, w)
    ref = b3[:, blk:blk + 1, :] if rev else b3[:, blk - 1:blk, :]
    return jnp.broadcast_to(ref, b3.shape).reshape(c, w)


def _hg_body(rows, qf, ff, vf, qb, fb, vb, lb_ref, of_ref, ob_ref, s_ref):
    c = CHUNK
    dirs = []
    for d, (q_ref, f_ref, v_ref) in enumerate(((qf, ff, vf), (qb, fb, vb))):
        rev = d == 1
        lb = lb_ref[d:d + 1, :]
        half_span = 0.5 * (1.0 - lb)
        t_f = jnp.tanh(f_ref[0, rows[d], :]) * half_span
        q_all = _silu_of_half(q_ref[0, rows[d], :].astype(_F32))
        k_all = half_span - t_f
        f_all = (lb + half_span) + t_f
        logf = jnp.log(f_all)
        b_all = _cumsum_rows(logf, rev)
        btot = jnp.sum(logf, axis=0, keepdims=True)
        dirs.append(dict(rev=rev, q=q_all, k=k_all, f=f_all, v=v_ref[0, rows[d], :],
                         b2=b_all * LOG2_E,
                         qe=q_all * jnp.exp(b_all), ke=k_all * jnp.exp(btot - b_all), e_tot=jnp.exp(btot),
                         levels=_hg_level_masks(c, rev), bands=_hg_band_masks(c, rev)))
        yield
    lvl_ops, band_sums = [], []
    for d, h in _CHAINS:
        g = dirs[d]
        rev = g["rev"]
        q = _head(g["q"], h)
        k = _head(g["k"], h)
        f = _head(g["f"], h)
        b2 = _head(g["b2"], h)
        q16 = q.astype(_BF16)
        k16 = k.astype(_BF16)
        ops = []
        for blk, _ in g["levels"]:
            e = jnp.exp2(-jnp.abs(b2 - _pair_reference(b2, blk, rev))).astype(_BF16)
            ops.append((q16 * e, k16 * e))
        lvl_ops.append(ops)
        sums = []
        qd = q
        for dl in range(SUB):
            if dl > 0:
                qd = qd * (f if dl == 1 else _group_roll(f, (1 - dl) if rev else (dl - 1)))
            k_dl = k if dl == 0 else _group_roll(k, -dl if rev else dl)
            sums.append(jnp.sum(qd * k_dl, axis=-1, keepdims=True))
        band_sums.append(sums)
        if h % 2 == 1:
            yield
    prods = []
    for i in range(len(_CHAINS)):
        prods.append([_bdot_nt(qs, ks) for qs, ks in lvl_ops[i]])
        if i % 2 == 1:
            yield
    attn = []
    for i, (d, h) in enumerate(_CHAINS):
        g = dirs[d]
        a = jnp.zeros((c, c), _F32)
        for (_, mask), prod in zip(g["levels"], prods[i]):
            a = jnp.where(mask, prod, a)
        for dl in range(SUB):
            a = jnp.where(g["bands"][dl], band_sums[i][dl], a)
        attn.append(a)
    yield
    outs = []
    for i, (d, h) in enumerate(_CHAINS):
        g = dirs[d]
        state_t = s_ref[d, h]
        outs.append(_bdot_nt(_head(g["qe"], h), state_t) + _bdot(attn[i], _head(g["v"], h)))
        s_ref[d, h] = state_t * _head(g["e_tot"], h) + _bdot_tn(_head(g["v"], h), _head(g["ke"], h))
        if h % 2 == 1:
            yield
    of_ref[0, rows[0], :] = jnp.concatenate(outs[:N_HEADS], axis=1).astype(_BF16)
    ob_ref[0, rows[1], :] = jnp.concatenate(outs[N_HEADS:], axis=1).astype(_BF16)


def _ml_body(rows, af, vf, smt_f, ab, vb, smt_b, gp_ref, of_ref, ob_ref, c_ref, m_ref):
    c = CHUNK
    n_g = 2 * N_HEADS
    lane = lax.broadcasted_iota(jnp.int32, (c, HEAD_DIM), 1)
    ones_col = jnp.where(lane == 0, 1.0, 0.0).astype(_BF16)
    gates = []
    for rev, smt in ((False, smt_f), (True, smt_b)):
        ig = smt[SM_IG:SM_IG + n_g] + gp_ref[0:n_g]
        f_t = smt[SM_FG:SM_FG + n_g] + gp_ref[n_g:2 * n_g]
        logf = jnp.minimum(f_t, 0.0) - jnp.log(1.0 + jnp.exp(-jnp.abs(f_t)))
        bcum = _cumsum_lanes(logf, rev)
        cols = jnp.concatenate([ig, bcum, jnp.zeros((HEAD_DIM - 2 * n_g, c), _F32)], axis=0).T
        gates.append(dict(ig=ig, bcum=bcum, btot=jnp.sum(logf, axis=1, keepdims=True), cols=cols,
                          incl=_tri_masks(c, rev)[0]))
    acts = (af, ab)
    vals = (vf, vb)
    q, k, v_ext, qk, qs = [], [], [], [], []
    for d, h in _CHAINS:
        qh = acts[d][0, rows[d], h * HEAD_DIM:(h + 1) * HEAD_DIM]
        kh = acts[d][0, rows[d], MIX_WIDTH + h * HEAD_DIM:MIX_WIDTH + (h + 1) * HEAD_DIM]
        vh = vals[d][0, rows[d], h * HEAD_DIM:(h + 1) * HEAD_DIM]
        q.append(qh)
        k.append(kh)
        v_ext.append(jnp.concatenate([vh.astype(_BF16), ones_col], axis=1))
        qk.append(_bdot_nt(qh, kh))
        qs.append(_bdot(qh, c_ref[d, h]))
        if h == N_HEADS - 1:
            yield
    n_ch = len(_CHAINS)
    logw, inter, logw_end, bt_m = [], [], [], []
    for d, h in _CHAINS:
        g = gates[d]
        col = d * N_HEADS + h
        b_col = g["cols"][:, n_g + col:n_g + col + 1]
        i_col = g["cols"][:, col:col + 1]
        b_row = g["bcum"][col:col + 1, :]
        i_row = g["ig"][col:col + 1, :]
        bt = g["btot"][col:col + 1, :]
        m_old = m_ref[col:col + 1, 0:1]
        logw.append(jnp.where(g["incl"], b_col - b_row + i_row, NEG_BIG))
        inter.append(b_col + m_old)
        logw_end.append(bt - b_col + i_col)
        bt_m.append(bt + m_old)
    yield
    m_row = [jnp.maximum(inter[i], jnp.max(logw[i], axis=-1, keepdims=True)) for i in range(n_ch)]
    m_new = [jnp.maximum(bt_m[i], jnp.max(logw_end[i], axis=0, keepdims=True)) for i in range(n_ch)]
    yield
    p = [qk[i] * jnp.exp(logw[i] - m_row[i]) for i in range(n_ch)]
    yield
    pv = [_bdot(p[i], v_ext[i]) for i in range(n_ch)]
    yield
    outs = []
    for i in range(n_ch):
        nd = jnp.exp(inter[i] - m_row[i]) * qs[i] + pv[i]
        den = nd[:, HEAD_DIM:HEAD_DIM + 1]
        outs.append(nd[:, :HEAD_DIM] / jnp.maximum(jnp.abs(den), jnp.exp(-m_row[i])))
    yield
    kw = [k[i].astype(_F32) * jnp.exp(logw_end[i] - m_new[i]) for i in range(n_ch)]
    for i, (d, h) in enumerate(_CHAINS):
        col = d * N_HEADS + h
        c_ref[d, h] = jnp.exp(bt_m[i] - m_new[i]) * c_ref[d, h] + _bdot_tn(kw[i], v_ext[i])
        m_ref[col:col + 1, :] = jnp.broadcast_to(m_new[i], (1, HEAD_DIM))
    of_ref[0, rows[0], :] = jnp.concatenate(outs[:N_HEADS], axis=1).astype(_BF16)
    ob_ref[0, rows[1], :] = jnp.concatenate(outs[N_HEADS:], axis=1).astype(_BF16)


def _mix_kernel(sf, sb, dn_af, dn_ab, dn_gp, hg_qf, hg_ff, hg_vf, hg_qb, hg_fb, hg_vb, hg_lb,
                ml_af, ml_vf, ml_ab, ml_vb, ml_gp,
                dn_of, dn_ob, hg_of, hg_ob, ml_of, ml_ob, dn_s, hg_s, ml_c, ml_m):
    @pl.when(pl.program_id(1) == 0)
    def _():
        dn_s[...] = jnp.zeros_like(dn_s)
        hg_s[...] = jnp.zeros_like(hg_s)
        ml_c[...] = jnp.zeros_like(ml_c)
        ml_m[...] = jnp.full(ml_m.shape, NEG_BIG, _F32)

    n_sub = sf.shape[1] // CHUNK

    def chunk_step(j, carry):
        rows = (pl.ds(pl.multiple_of(j * CHUNK, CHUNK), CHUNK),
                pl.ds(pl.multiple_of((n_sub - 1 - j) * CHUNK, CHUNK), CHUNK))
        smt_f = sf[0, rows[0], :].T
        smt_b = sb[0, rows[1], :].T
        dn = _dn_body(rows, dn_af, smt_f, dn_ab, smt_b, dn_gp, dn_of, dn_ob, dn_s)
        hg = _hg_body(rows, hg_qf, hg_ff, hg_vf, hg_qb, hg_fb, hg_vb, hg_lb, hg_of, hg_ob, hg_s)
        ml = _ml_body(rows, ml_af, ml_vf, smt_f, ml_ab, ml_vb, smt_b, ml_gp, ml_of, ml_ob, ml_c, ml_m)
        pending = [dn, hg, dn, ml]
        while pending:
            for gen in list(pending):
                if gen in pending and next(gen, _DONE) is _DONE:
                    pending = [g for g in pending if g is not gen]
        return carry

    lax.fori_loop(0, n_sub, chunk_step, 0)


def _mix_call(zb3, zf3, dn_act, ml_act, dn_gates, hg_lb, ml_gates):
    b, s, _ = zb3.shape
    c = min(MIX_ROWS, s)
    n_chunks = s // c
    pair = lambda width, off: _pair_specs(c, n_chunks, width, off)
    sm, dn_a, ml_a = pair(HEAD_DIM, SM_OFF), pair(DN_ACT_W, 0), pair(ML_ACT_W, 0)
    hg_q, hg_v, ml_v = pair(MIX_WIDTH, HG_Q_OFF), pair(MIX_WIDTH, HG_I_OFF), pair(MIX_WIDTH, ML_V_OFF)
    hg_ff = pair(MIX_WIDTH, HG_F_OFF)[0]
    hg_fb = pair(MIX_WIDTH, HG_F_OFF + MIX_WIDTH)[1]
    in_specs = [sm[0], sm[1], dn_a[0], dn_a[1], _const_spec(dn_gates.shape),
                hg_q[0], hg_ff, hg_v[0], hg_q[1], hg_fb, hg_v[1], _const_spec(hg_lb.shape),
                ml_a[0], ml_v[0], ml_a[1], ml_v[1], _const_spec(ml_gates.shape)]
    args = (zf3, zf3, dn_act, dn_act, dn_gates, zb3, zf3, zb3, zb3, zf3, zb3, hg_lb,
            ml_act, zb3, ml_act, zb3, ml_gates)
    out = jax.ShapeDtypeStruct((b, s, MIX_WIDTH), _BF16)
    state = (2, N_HEADS, HEAD_DIM, HEAD_DIM)
    return pl.pallas_call(
        _mix_kernel,
        grid=(b, n_chunks),
        in_specs=in_specs,
        out_specs=pair(MIX_WIDTH, 0) * 3,
        out_shape=[out] * 6,
        scratch_shapes=[pltpu.VMEM(state, _F32), pltpu.VMEM(state, _F32),
                        pltpu.VMEM((2, N_HEADS, HEAD_DIM, 2 * HEAD_DIM), _F32),
                        pltpu.VMEM((2 * N_HEADS, HEAD_DIM), _F32)],
        compiler_params=pltpu.CompilerParams(dimension_semantics=("arbitrary", "arbitrary"),
                                             vmem_limit_bytes=VMEM_LIMIT),
        name="mixers",
    )(*args)


def _head_norm(o, w):
    parts = []
    for h in range(N_HEADS):
        seg = _head(o, h)
        parts.append(seg * lax.rsqrt(jnp.mean(seg * seg, axis=-1, keepdims=True) + EPS))
    return jnp.concatenate(parts, axis=1) * w


def _merge_kernel(x_ref, dnf, dnb, hgf, hgb, mlf, mlb, dnz, hgz, mlo, gp_ref,
                  nw_ref, wbr_ref, wout_ref, gpost_ref, o_ref):
    both = lambda f, b: f[...].astype(_F32) + b[...].astype(_F32)
    branches = (
        _head_norm(both(dnf, dnb), nw_ref[0:1, :]) * _silu_of_half(dnz[...].astype(_F32)),
        _head_norm(both(hgf, hgb), nw_ref[1:2, :]) * _silu_of_half(hgz[...].astype(_F32)),
        _head_norm(both(mlf, mlb), nw_ref[2:3, :]) * _two_sigmoid_of_half(mlo[...].astype(_F32)),
    )
    merged = None
    for i, br in enumerate(branches):
        gate = _two_sigmoid_of_half(gp_ref[:, i * D_MODEL:(i + 1) * D_MODEL].astype(_F32))
        term = gate * _bdot(br, wbr_ref[i])
        merged = term if merged is None else merged + term
    y = _bdot(merged, wout_ref[...])
    o_ref[...] = x_ref[...] + _rmsnorm(y, gpost_ref[...])


def _merge_call(x2, zb2, mix_outs, norm_w, w_branch, w_out, g_post):
    t = x2.shape[0]
    tm = min(MERGE_TM, t)
    row = lambda w: pl.BlockSpec((tm, w), lambda i: (i, 0))
    zblk = lambda w, off: pl.BlockSpec((tm, w), lambda i: (i, off // w))
    full = lambda a: pl.BlockSpec(a.shape, lambda i: (0,) * a.ndim)
    in_specs = ([row(D_MODEL)] + [row(MIX_WIDTH)] * 6
                + [zblk(MIX_WIDTH, DN_Z_OFF), zblk(MIX_WIDTH, HG_Z_OFF), zblk(MIX_WIDTH, ML_O_OFF),
                   zblk(N_BRANCH * D_MODEL, GP_OFF)]
                + [full(norm_w), full(w_branch), full(w_out), full(g_post)])
    return pl.pallas_call(
        _merge_kernel,
        grid=(t // tm,),
        in_specs=in_specs,
        out_specs=row(D_MODEL),
        out_shape=jax.ShapeDtypeStruct((t, D_MODEL), _F32),
        compiler_params=pltpu.CompilerParams(dimension_semantics=("arbitrary",),
                                             vmem_limit_bytes=VMEM_LIMIT),
        name="merge",
    )(x2, *mix_outs, zb2, zb2, zb2, zb2, norm_w, w_branch, w_out, g_post)


def _ffn_kernel(x_ref, gpre_ref, w1_ref, w2_ref, gpost_ref, o_ref):
    x = x_ref[...]
    u = jnp.maximum(_bdot(_rmsnorm(x, gpre_ref[...]), w1_ref[...]), 0.0)
    y = _bdot(u * u, w2_ref[...])
    o_ref[...] = x + _rmsnorm(y, gpost_ref[...])


def _ffn_call(x2, g_pre, w1, w2, g_post):
    t = x2.shape[0]
    tm = min(FFN_TM, t)
    row = pl.BlockSpec((tm, D_MODEL), lambda i: (i, 0))
    full = lambda a: pl.BlockSpec(a.shape, lambda i: (0,) * a.ndim, pipeline_mode=pl.Buffered(1))
    return pl.pallas_call(
        _ffn_kernel,
        grid=(t // tm,),
        in_specs=[row, full(g_pre), full(w1), full(w2), full(g_post)],
        out_specs=row,
        out_shape=jax.ShapeDtypeStruct((t, D_MODEL), _F32),
        compiler_params=pltpu.CompilerParams(dimension_semantics=("arbitrary",),
                                             vmem_limit_bytes=VMEM_LIMIT),
        name="ffn",
    )(x2, g_pre, w1, w2, g_post)


def _permute_w_in(w_in):
    m = MIX_WIDTH
    o_dn, o_sm_dn, o_hg, o_ml, o_sm_ml, o_gp = 0, 4 * m, 4 * m + 16, 9 * m + 16, 13 * m + 16, 13 * m + 32
    wide = [
        0.5 * w_in[:, o_gp:o_gp + N_BRANCH * D_MODEL],
        w_in[:, o_dn:o_dn + 3 * m],
        0.5 * w_in[:, o_dn + 3 * m:o_dn + 4 * m],
        w_in[:, o_ml:o_ml + 3 * m],
        0.5 * w_in[:, o_ml + 3 * m:o_ml + 4 * m],
        0.5 * w_in[:, o_hg:o_hg + m],
        w_in[:, o_hg + 3 * m:o_hg + 4 * m],
        0.5 * w_in[:, o_hg + 4 * m:o_hg + 5 * m],
        jnp.zeros((w_in.shape[0], NB_COLS - HG_Z_OFF - m), w_in.dtype),
    ]
    gates = [
        0.5 * w_in[:, o_hg + m:o_hg + 3 * m],
        w_in[:, o_sm_dn:o_sm_dn + 16],
        w_in[:, o_sm_ml:o_sm_ml + 16],
        jnp.zeros((w_in.shape[0], NF_COLS - SM_OFF - 32), w_in.dtype),
    ]
    return jnp.concatenate(wide, axis=1).astype(_BF16), jnp.concatenate(gates, axis=1).astype(_BF16)


def _gate_rows(first, second):
    vals = jnp.concatenate([first.astype(_F32).reshape(-1), second.astype(_F32).reshape(-1)])
    return jnp.broadcast_to(vals[:, None], (vals.shape[0], HEAD_DIM))


def _layer(x3, p):
    b, s, _ = x3.shape
    x2 = x3.reshape(b * s, D_MODEL)
    w_wide, w_gates = p["w_in"]
    zb2 = _proj(x2, p["norm_mix_pre"], w_wide, _BF16)
    zb3 = zb2.reshape(b, s, NB_COLS)
    zf3 = _proj(x2, p["norm_mix_pre"], w_gates, _F32).reshape(b, s, NF_COLS)
    dn_act, ml_act = _prep_call(zb3, p["dn_conv"], p["ml_conv"])
    mix = _mix_call(zb3, zf3, dn_act, ml_act, p["dn_gates"], p["hg_lb"], p["ml_gates"])
    mix = [a.reshape(b * s, MIX_WIDTH) for a in mix]
    x2 = _merge_call(x2, zb2, mix, p["mix_norm"], p["w_branch"], p["w_out"], p["norm_mix_post"])
    x2 = _ffn_call(x2, p["norm_ffn_pre"], p["w_ff1"], p["w_ff2"], p["norm_ffn_post"])
    return x2.reshape(b, s, D_MODEL)


def _layer_params(l, lower_bounds, norm_mix_pre, norm_mix_post, norm_ffn_pre, norm_ffn_post, w_in, dn_conv,
                  dn_a_log, dn_dt_bias, dn_norm, hg_norm, ml_conv, ml_i_bias, ml_f_bias, ml_norm,
                  w_branch, w_out, w_ff1, w_ff2):
    return {
        "norm_mix_pre": norm_mix_pre[l].reshape(1, D_MODEL),
        "norm_mix_post": norm_mix_post[l].reshape(1, D_MODEL),
        "norm_ffn_pre": norm_ffn_pre[l].reshape(1, D_MODEL),
        "norm_ffn_post": norm_ffn_post[l].reshape(1, D_MODEL),
        "w_in": _permute_w_in(w_in[l]),
        "dn_conv": 0.5 * dn_conv[l],
        "dn_gates": _gate_rows(dn_dt_bias[l], dn_a_log[l]),
        "hg_lb": lower_bounds[l],
        "ml_conv": 0.5 * ml_conv[l],
        "ml_gates": _gate_rows(ml_i_bias[l], ml_f_bias[l]),
        "mix_norm": jnp.stack([dn_norm[l].reshape(-1), hg_norm[l].reshape(-1), 0.5 * ml_norm[l].reshape(-1)]
                              + [jnp.zeros((MIX_WIDTH,), _F32)] * 5),
        "w_branch": (0.5 * w_branch[l]).astype(_BF16),
        "w_out": w_out[l].astype(_BF16),
        "w_ff1": w_ff1[l].astype(_BF16),
        "w_ff2": w_ff2[l].astype(_BF16),
    }


def kernel(x_prompt, x_sample, norm_mix_pre, norm_mix_post, norm_ffn_pre, norm_ffn_post, w_in, dn_conv,
           dn_a_log, dn_dt_bias, dn_norm, hg_lb, hg_norm, ml_conv, ml_i_bias, ml_f_bias, ml_norm,
           w_branch, w_out, w_ff1, w_ff2):
    p_layers = jax.nn.softmax(hg_lb.astype(_F32), axis=0)
    lower_bounds = jnp.cumsum(p_layers, axis=0) - p_layers[0]
    y_prompt, y_sample = x_prompt, x_sample
    for l in range(w_in.shape[0]):
        p = _layer_params(l, lower_bounds, norm_mix_pre, norm_mix_post, norm_ffn_pre, norm_ffn_post, w_in,
                          dn_conv, dn_a_log, dn_dt_bias, dn_norm, hg_norm, ml_conv, ml_i_bias, ml_f_bias,
                          ml_norm, w_branch, w_out, w_ff1, w_ff2)
        y_prompt = _layer(y_prompt, p)
        y_sample = _layer(y_sample, p)
    return (y_prompt, y_sample)
```

```python
import jax
import jax.numpy as jnp
from jax import lax
from jax.experimental import pallas as pl
from jax.experimental.pallas import tpu as pltpu

D_MODEL = 1024
N_HEADS = 4
HEAD_DIM = 128
MIX_WIDTH = N_HEADS * HEAD_DIM
N_BRANCH = 3
CONV_K = 5
CONV_HALO = 8
HALO_ROWS = 16
D_FF = 4 * D_MODEL
EPS = 1e-6
LOG2_E = 1.4426950408889634
NEG_BIG = -1e30
CHUNK = 128
MIX_ROWS = 4 * CHUNK
SUB = 4
SOLVE_BASE = 16

GP_OFF = 0
DN_QKV_OFF = 3072
DN_Z_OFF = 4608
ML_QK_OFF = 5120
ML_V_OFF = 6144
ML_O_OFF = 6656
HG_Q_OFF = 7168
HG_I_OFF = 7680
HG_Z_OFF = 8192
NB_COLS = 8960
HG_F_OFF = 0
SM_OFF = 1024
NF_COLS = 1152
SM_BETA, SM_ALPHA, SM_IG, SM_FG = 0, 8, 16, 24
DN_ACT_W = 3 * MIX_WIDTH
ML_ACT_W = 2 * MIX_WIDTH

PROJ_TM = 512
PREP_TM = 256
MERGE_TM = 512
FFN_TM = 512
VMEM_LIMIT = 56 * 1024 * 1024

_F32 = jnp.float32
_BF16 = jnp.bfloat16
_CHAINS = tuple((d, h) for d in range(2) for h in range(N_HEADS))
_DONE = object()


def _bdot(a, b):
    return jnp.dot(a.astype(_BF16), b.astype(_BF16), preferred_element_type=_F32)


def _bdot_nt(a, b):
    return lax.dot_general(a.astype(_BF16), b.astype(_BF16), (((1,), (1,)), ((), ())),
                           preferred_element_type=_F32)


def _bdot_tn(a, b):
    return lax.dot_general(a.astype(_BF16), b.astype(_BF16), (((0,), (0,)), ((), ())),
                           preferred_element_type=_F32)


def _sigmoid(x):
    return 0.5 * jnp.tanh(0.5 * x) + 0.5


def _silu_of_half(h):
    return h * (jnp.tanh(h) + 1.0)


def _two_sigmoid_of_half(h):
    return jnp.tanh(h) + 1.0


def _softplus(x):
    return jnp.maximum(x, 0.0) + jnp.log(1.0 + jnp.exp(-jnp.abs(x)))


def _rmsnorm(x, g):
    return x * lax.rsqrt(jnp.mean(x * x, axis=-1, keepdims=True) + EPS) * g


def _split3(x):
    mask = jnp.uint32(0xFFFF0000)

    def top(v):
        return lax.bitcast_convert_type(lax.bitcast_convert_type(v, jnp.uint32) & mask, _F32)

    hi = top(x)
    r1 = x - hi
    mid = top(r1)
    return hi.astype(_BF16), mid.astype(_BF16), (r1 - mid).astype(_BF16)


def _cumsum_rows(x, rev):
    c, w = x.shape
    row = lax.broadcasted_iota(jnp.int32, (c, c), 0)
    col = lax.broadcasted_iota(jnp.int32, (c, c), 1)
    tri = jnp.where((col >= row) if rev else (col <= row), 1.0, 0.0).astype(_BF16)
    s = jnp.dot(tri, jnp.concatenate(_split3(x), axis=1), preferred_element_type=_F32)
    return s[:, :w] + s[:, w:2 * w] + s[:, 2 * w:]


def _cumsum_lanes(x, rev):
    r, c = x.shape
    row = lax.broadcasted_iota(jnp.int32, (c, c), 0)
    col = lax.broadcasted_iota(jnp.int32, (c, c), 1)
    tri = jnp.where((row >= col) if rev else (row <= col), 1.0, 0.0).astype(_BF16)
    s = jnp.dot(jnp.concatenate(_split3(x), axis=0), tri, preferred_element_type=_F32)
    return s[:r] + s[r:2 * r] + s[2 * r:]


def _head(t, h):
    return t[:, h * HEAD_DIM:(h + 1) * HEAD_DIM]


def _proj_kernel(x_ref, g_ref, wb_ref, wf_ref, ob_ref, of_ref):
    h = _rmsnorm(x_ref[...], g_ref[...]).astype(_BF16)
    ob_ref[...] = jnp.dot(h, wb_ref[...], preferred_element_type=_F32).astype(_BF16)
    of_ref[...] = jnp.dot(h, wf_ref[...], preferred_element_type=_F32)


def _proj(x2, g, w_wide, w_gates):
    t = x2.shape[0]
    tm = min(PROJ_TM, t)
    resident = lambda a: pl.BlockSpec(a.shape, lambda i: (0, 0), pipeline_mode=pl.Buffered(1))
    return pl.pallas_call(
        _proj_kernel,
        grid=(t // tm,),
        in_specs=[pl.BlockSpec((tm, D_MODEL), lambda i: (i, 0)), resident(g), resident(w_wide), resident(w_gates)],
        out_specs=[pl.BlockSpec((tm, NB_COLS), lambda i: (i, 0)), pl.BlockSpec((tm, NF_COLS), lambda i: (i, 0))],
        out_shape=[jax.ShapeDtypeStruct((t, NB_COLS), _BF16), jax.ShapeDtypeStruct((t, NF_COLS), _F32)],
        compiler_params=pltpu.CompilerParams(dimension_semantics=("arbitrary",), vmem_limit_bytes=VMEM_LIMIT),
        name="proj",
    )(x2, g, w_wide, w_gates)


def _conv_act(prev8, cur, next8, w, first, last):
    c = cur.shape[0]
    prev8 = jnp.where(first, 0.0, prev8[HALO_ROWS - CONV_HALO:].astype(_F32))
    next8 = jnp.where(last, 0.0, next8[:CONV_HALO].astype(_F32))
    ext = jnp.concatenate([prev8, cur.astype(_F32), next8], axis=0)
    rows = c + 2 * CONV_HALO
    y = None
    for j in range(CONV_K):
        sh = (CONV_K // 2 - j) % rows
        tap = ext if sh == 0 else pltpu.roll(ext, sh, 0)
        term = w[j:j + 1] * tap[CONV_HALO:CONV_HALO + c]
        y = term if y is None else y + term
    return _silu_of_half(y)


def _prep_kernel(dq, dp, dn, mq, mp, mn, dw_ref, mw_ref, dn_out, ml_out):
    n = pl.program_id(1)
    first = n == 0
    last = n == pl.num_programs(1) - 1
    for h in range(3 * N_HEADS):
        sl = slice(h * HEAD_DIM, (h + 1) * HEAD_DIM)
        t = _conv_act(dp[0, :, sl], dq[0, :, sl], dn[0, :, sl], dw_ref[:, sl], first, last)
        if h < 2 * N_HEADS:
            inv = lax.rsqrt(jnp.sum(t * t, axis=-1, keepdims=True) + EPS)
            t = t * (inv * (HEAD_DIM ** -0.5) if h < N_HEADS else inv)
        dn_out[0, :, sl] = t.astype(_BF16)
    for h in range(2 * N_HEADS):
        sl = slice(h * HEAD_DIM, (h + 1) * HEAD_DIM)
        t = _conv_act(mp[0, :, sl], mq[0, :, sl], mn[0, :, sl], mw_ref[:, sl], first, last)
        ml_out[0, :, sl] = (t * (HEAD_DIM ** -0.5) if h < N_HEADS else t).astype(_BF16)


def _halo_specs(tm, seq, width, off):
    blk = off // width
    r8 = tm // HALO_ROWS
    last8 = seq // HALO_ROWS - 1
    return [pl.BlockSpec((1, tm, width), lambda b, n: (b, n, blk)),
            pl.BlockSpec((1, HALO_ROWS, width), lambda b, n: (b, jnp.maximum(n * r8 - 1, 0), blk)),
            pl.BlockSpec((1, HALO_ROWS, width), lambda b, n: (b, jnp.minimum((n + 1) * r8, last8), blk))]


def _prep_call(zb3, dn_conv, ml_conv):
    b, s, _ = zb3.shape
    tm = min(PREP_TM, s)
    in_specs = (_halo_specs(tm, s, DN_ACT_W, DN_QKV_OFF) + _halo_specs(tm, s, ML_ACT_W, ML_QK_OFF)
                + [_const_spec(dn_conv.shape), _const_spec(ml_conv.shape)])
    return pl.pallas_call(
        _prep_kernel,
        grid=(b, s // tm),
        in_specs=in_specs,
        out_specs=[pl.BlockSpec((1, tm, DN_ACT_W), lambda b, n: (b, n, 0)),
                   pl.BlockSpec((1, tm, ML_ACT_W), lambda b, n: (b, n, 0))],
        out_shape=[jax.ShapeDtypeStruct((b, s, DN_ACT_W), _BF16),
                   jax.ShapeDtypeStruct((b, s, ML_ACT_W), _BF16)],
        compiler_params=pltpu.CompilerParams(dimension_semantics=("arbitrary", "arbitrary"),
                                             vmem_limit_bytes=VMEM_LIMIT),
        name="prep",
    )(zb3, zb3, zb3, zb3, zb3, zb3, dn_conv, ml_conv)


def _pair_specs(c, n_chunks, width, off):
    blk = off // width
    return [pl.BlockSpec((1, c, width), lambda b, n: (b, n, blk)),
            pl.BlockSpec((1, c, width), lambda b, n: (b, n_chunks - 1 - n, blk))]


def _const_spec(shape):
    nd = len(shape)
    return pl.BlockSpec(shape, lambda b, n: (0,) * nd)


def _tri_masks(c, rev):
    row = lax.broadcasted_iota(jnp.int32, (c, c), 0)
    col = lax.broadcasted_iota(jnp.int32, (c, c), 1)
    return ((col >= row), (col > row)) if rev else ((col <= row), (col < row))


def _unit_tri_inverse_many(mats):
    c = mats[0].shape[0]
    row = lax.broadcasted_iota(jnp.int32, (c, c), 0)
    col = lax.broadcasted_iota(jnp.int32, (c, c), 1)
    blk = min(SOLVE_BASE, c)
    eye = jnp.where(row == col, 1.0, 0.0).astype(_F32)
    diag_blk = (row // blk) == (col // blk)
    a0 = [jnp.where(diag_blk, a, 0.0) for a in mats]
    ts = [eye - x for x in a0]
    pws = [_bdot(x, x) for x in a0]
    yield
    k = 2
    while 2 * k < blk:
        boths = [_bdot(jnp.concatenate([t, pw], axis=0), pw) for t, pw in zip(ts, pws)]
        ts = [t + both[:c] for t, both in zip(ts, boths)]
        pws = [both[c:] for both in boths]
        k *= 2
        yield
    ts = [t + _bdot(t, pw) for t, pw in zip(ts, pws)]
    yield
    while blk < c:
        off = ((row // (2 * blk)) == (col // (2 * blk))) & ((row // blk) != (col // blk))
        tl = [_bdot(t, jnp.where(off, a, 0.0)) for t, a in zip(ts, mats)]
        yield
        ts = [t - _bdot(x, t) for t, x in zip(ts, tl)]
        yield
        blk *= 2
    return ts


def _dn_body(rows, af, smt_f, ab, smt_b, gp_ref, of_ref, ob_ref, s_ref):
    c = CHUNK
    n_g = 2 * N_HEADS
    gates = []
    for rev, smt in ((False, smt_f), (True, smt_b)):
        beta = _sigmoid(smt[SM_BETA:SM_BETA + n_g])
        g = -jnp.exp(gp_ref[n_g:2 * n_g]) * _softplus(smt[SM_ALPHA:SM_ALPHA + n_g] + gp_ref[0:n_g])
        gcum = _cumsum_lanes(g, rev)
        gtot = jnp.sum(g, axis=1, keepdims=True)
        stack = jnp.concatenate([beta, gcum, jnp.exp(gcum), jnp.exp(gtot - gcum),
                                 jnp.zeros((HEAD_DIM - 4 * n_g, c), _F32)], axis=0)
        gates.append(dict(cols=stack.T, gcum=gcum, e_tot=jnp.exp(gtot), masks=_tri_masks(c, rev)))
    yield
    acts = (af, ab)
    q, k, kb, vb, eg, a_low, attn = [], [], [], [], [], [], []
    for d, h in _CHAINS:
        g = gates[d]
        r = d * N_HEADS + h
        cols = g["cols"]
        incl, strict = g["masks"]
        qh = acts[d][0, rows[d], h * HEAD_DIM:(h + 1) * HEAD_DIM]
        kh = acts[d][0, rows[d], MIX_WIDTH + h * HEAD_DIM:MIX_WIDTH + (h + 1) * HEAD_DIM]
        vh = acts[d][0, rows[d], 2 * MIX_WIDTH + h * HEAD_DIM:2 * MIX_WIDTH + (h + 1) * HEAD_DIM]
        beta = cols[:, r:r + 1]
        decay = jnp.exp(jnp.where(incl, cols[:, n_g + r:n_g + r + 1] - g["gcum"][r:r + 1, :], NEG_BIG))
        kbh = kh.astype(_F32) * beta
        both = _bdot_nt(jnp.concatenate([kbh.astype(_BF16), qh], axis=0), kh)
        a_low.append(jnp.where(strict, both[:c] * decay, 0.0))
        attn.append(both[c:] * decay)
        q.append(qh)
        k.append(kh)
        kb.append(kbh)
        vb.append(vh.astype(_F32) * beta)
        eg.append(cols[:, 2 * n_g + r:2 * n_g + r + 1])
        if h == N_HEADS - 1:
            yield
    tinv = yield from _unit_tri_inverse_many(a_low)
    sol = [_bdot(t, jnp.concatenate([vb[i], kb[i] * eg[i]], axis=1)) for i, t in enumerate(tinv)]
    yield
    ws_qs = [_bdot(jnp.concatenate([sol[i][:, HEAD_DIM:], q[i].astype(_F32) * eg[i]], axis=0), s_ref[d, h])
             for i, (d, h) in enumerate(_CHAINS)]
    yield
    v_new = [sol[i][:, :HEAD_DIM] - ws_qs[i][:c] for i in range(len(_CHAINS))]
    outs = [ws_qs[i][c:] + _bdot(attn[i], v_new[i]) for i in range(len(_CHAINS))]
    yield
    for i, (d, h) in enumerate(_CHAINS):
        r = d * N_HEADS + h
        er = gates[d]["cols"][:, 3 * n_g + r:3 * n_g + r + 1]
        et = gates[d]["e_tot"][r:r + 1, :]
        s_ref[d, h] = s_ref[d, h] * et + _bdot_tn(k[i].astype(_F32) * er, v_new[i])
    of_ref[0, rows[0], :] = jnp.concatenate(outs[:N_HEADS], axis=1).astype(_BF16)
    ob_ref[0, rows[1], :] = jnp.concatenate(outs[N_HEADS:], axis=1).astype(_BF16)


def _hg_level_masks(c, rev):
    row = lax.broadcasted_iota(jnp.int32, (c, c), 0)
    col = lax.broadcasted_iota(jnp.int32, (c, c), 1)
    levels = []
    blk = c // 2
    while blk >= SUB:
        same_pair = (row // (2 * blk)) == (col // (2 * blk))
        row_hi = (row // blk) % 2 == 1
        col_hi = (col // blk) % 2 == 1
        mask = (same_pair & (~row_hi) & col_hi) if rev else (same_pair & row_hi & (~col_hi))
        levels.append((blk, mask))
        blk //= 2
    return levels


def _hg_band_masks(c, rev):
    row = lax.broadcasted_iota(jnp.int32, (c, c), 0)
    col = lax.broadcasted_iota(jnp.int32, (c, c), 1)
    masks = []
    for dl in range(SUB):
        if rev:
            masks.append((col == row + dl) & ((row % SUB) + dl < SUB))
        else:
            masks.append((col == row - dl) & ((row % SUB) >= dl))
    return masks


def _group_roll(x, shift):
    c, w = x.shape
    return pltpu.roll(x.reshape(c // 8, 8, w), shift % 8, 1).reshape(c, w)


def _pair_reference(b, blk, rev):
    c, w = b.shape
    b3 = b.reshape(c // (2 * blk), 2 * blk, w)
    ref = b3[:, blk:blk + 1, :] if rev else b3[:, blk - 1:blk, :]
    return jnp.broadcast_to(ref, b3.shape).reshape(c, w)


def _hg_body(rows, qf, ff, vf, qb, fb, vb, lb_ref, of_ref, ob_ref, s_ref):
    c = CHUNK
    dirs = []
    for d, (q_ref, f_ref, v_ref) in enumerate(((qf, ff, vf), (qb, fb, vb))):
        rev = d == 1
        lb = lb_ref[d:d + 1, :]
        half_span = 0.5 * (1.0 - lb)
        t_f = jnp.tanh(f_ref[0, rows[d], :]) * half_span
        q_all = _silu_of_half(q_ref[0, rows[d], :].astype(_F32))
        k_all = half_span - t_f
        f_all = (lb + half_span) + t_f
        logf = jnp.log(f_all)
        b_all = _cumsum_rows(logf, rev)
        btot = jnp.sum(logf, axis=0, keepdims=True)
        dirs.append(dict(rev=rev, q=q_all, k=k_all, f=f_all, v=v_ref[0, rows[d], :],
                         b2=b_all * LOG2_E,
                         qe=q_all * jnp.exp(b_all), ke=k_all * jnp.exp(btot - b_all), e_tot=jnp.exp(btot),
                         levels=_hg_level_masks(c, rev), bands=_hg_band_masks(c, rev)))
        yield
    lvl_ops, band_sums = [], []
    for d, h in _CHAINS:
        g = dirs[d]
        rev = g["rev"]
        q = _head(g["q"], h)
        k = _head(g["k"], h)
        f = _head(g["f"], h)
        b2 = _head(g["b2"], h)
        q16 = q.astype(_BF16)
        k16 = k.astype(_BF16)
        ops = []
        for blk, _ in g["levels"]:
            e = jnp.exp2(-jnp.abs(b2 - _pair_reference(b2, blk, rev))).astype(_BF16)
            ops.append((q16 * e, k16 * e))
        lvl_ops.append(ops)
        sums = []
        qd = q
        for dl in range(SUB):
            if dl > 0:
                qd = qd * (f if dl == 1 else _group_roll(f, (1 - dl) if rev else (dl - 1)))
            k_dl = k if dl == 0 else _group_roll(k, -dl if rev else dl)
            sums.append(jnp.sum(qd * k_dl, axis=-1, keepdims=True))
        band_sums.append(sums)
        if h % 2 == 1:
            yield
    prods = []
    for i in range(len(_CHAINS)):
        prods.append([_bdot_nt(qs, ks) for qs, ks in lvl_ops[i]])
        if i % 2 == 1:
            yield
    attn = []
    for i, (d, h) in enumerate(_CHAINS):
        g = dirs[d]
        a = jnp.zeros((c, c), _F32)
        for (_, mask), prod in zip(g["levels"], prods[i]):
            a = jnp.where(mask, prod, a)
        for dl in range(SUB):
            a = jnp.where(g["bands"][dl], band_sums[i][dl], a)
        attn.append(a)
    yield
    outs = []
    for i, (d, h) in enumerate(_CHAINS):
        g = dirs[d]
        state_t = s_ref[d, h]
        outs.append(_bdot_nt(_head(g["qe"], h), state_t) + _bdot(attn[i], _head(g["v"], h)))
        s_ref[d, h] = state_t * _head(g["e_tot"], h) + _bdot_tn(_head(g["v"], h), _head(g["ke"], h))
        if h % 2 == 1:
            yield
    of_ref[0, rows[0], :] = jnp.concatenate(outs[:N_HEADS], axis=1).astype(_BF16)
    ob_ref[0, rows[1], :] = jnp.concatenate(outs[N_HEADS:], axis=1).astype(_BF16)


def _ml_body(rows, af, vf, smt_f, ab, vb, smt_b, gp_ref, of_ref, ob_ref, c_ref, m_ref):
    c = CHUNK
    n_g = 2 * N_HEADS
    lane = lax.broadcasted_iota(jnp.int32, (c, HEAD_DIM), 1)
    ones_col = jnp.where(lane == 0, 1.0, 0.0).astype(_BF16)
    gates = []
    for rev, smt in ((False, smt_f), (True, smt_b)):
        ig = smt[SM_IG:SM_IG + n_g] + gp_ref[0:n_g]
        f_t = smt[SM_FG:SM_FG + n_g] + gp_ref[n_g:2 * n_g]
        logf = jnp.minimum(f_t, 0.0) - jnp.log(1.0 + jnp.exp(-jnp.abs(f_t)))
        bcum = _cumsum_lanes(logf, rev)
        cols = jnp.concatenate([ig, bcum, jnp.zeros((HEAD_DIM - 2 * n_g, c), _F32)], axis=0).T
        gates.append(dict(ig=ig, bcum=bcum, btot=jnp.sum(logf, axis=1, keepdims=True), cols=cols,
                          incl=_tri_masks(c, rev)[0]))
    acts = (af, ab)
    vals = (vf, vb)
    q, k, v_ext, qk, qs = [], [], [], [], []
    for d, h in _CHAINS:
        qh = acts[d][0, rows[d], h * HEAD_DIM:(h + 1) * HEAD_DIM]
        kh = acts[d][0, rows[d], MIX_WIDTH + h * HEAD_DIM:MIX_WIDTH + (h + 1) * HEAD_DIM]
        vh = vals[d][0, rows[d], h * HEAD_DIM:(h + 1) * HEAD_DIM]
        q.append(qh)
        k.append(kh)
        v_ext.append(jnp.concatenate([vh.astype(_BF16), ones_col], axis=1))
        qk.append(_bdot_nt(qh, kh))
        qs.append(_bdot(qh, c_ref[d, h]))
        if h == N_HEADS - 1:
            yield
    n_ch = len(_CHAINS)
    logw, inter, logw_end, bt_m = [], [], [], []
    for d, h in _CHAINS:
        g = gates[d]
        col = d * N_HEADS + h
        b_col = g["cols"][:, n_g + col:n_g + col + 1]
        i_col = g["cols"][:, col:col + 1]
        b_row = g["bcum"][col:col + 1, :]
        i_row = g["ig"][col:col + 1, :]
        bt = g["btot"][col:col + 1, :]
        m_old = m_ref[col:col + 1, 0:1]
        logw.append(jnp.where(g["incl"], b_col - b_row + i_row, NEG_BIG))
        inter.append(b_col + m_old)
        logw_end.append(bt - b_col + i_col)
        bt_m.append(bt + m_old)
    yield
    m_row = [jnp.maximum(inter[i], jnp.max(logw[i], axis=-1, keepdims=True)) for i in range(n_ch)]
    m_new = [jnp.maximum(bt_m[i], jnp.max(logw_end[i], axis=0, keepdims=True)) for i in range(n_ch)]
    yield
    p = [qk[i] * jnp.exp(logw[i] - m_row[i]) for i in range(n_ch)]
    yield
    pv = [_bdot(p[i], v_ext[i]) for i in range(n_ch)]
    yield
    outs = []
    for i in range(n_ch):
        nd = jnp.exp(inter[i] - m_row[i]) * qs[i] + pv[i]
        den = nd[:, HEAD_DIM:HEAD_DIM + 1]
        outs.append(nd[:, :HEAD_DIM] / jnp.maximum(jnp.abs(den), jnp.exp(-m_row[i])))
    yield
    kw = [k[i].astype(_F32) * jnp.exp(logw_end[i] - m_new[i]) for i in range(n_ch)]
    for i, (d, h) in enumerate(_CHAINS):
        col = d * N_HEADS + h
        c_ref[d, h] = jnp.exp(bt_m[i] - m_new[i]) * c_ref[d, h] + _bdot_tn(kw[i], v_ext[i])
        m_ref[col:col + 1, :] = jnp.broadcast_to(m_new[i], (1, HEAD_DIM))
    of_ref[0, rows[0], :] = jnp.concatenate(outs[:N_HEADS], axis=1).astype(_BF16)
    ob_ref[0, rows[1], :] = jnp.concatenate(outs[N_HEADS:], axis=1).astype(_BF16)


def _mix_kernel(sf, sb, dn_af, dn_ab, dn_gp, hg_qf, hg_ff, hg_vf, hg_qb, hg_fb, hg_vb, hg_lb,
                ml_af, ml_vf, ml_ab, ml_vb, ml_gp,
                dn_of, dn_ob, hg_of, hg_ob, ml_of, ml_ob, dn_s, hg_s, ml_c, ml_m):
    @pl.when(pl.program_id(1) == 0)
    def _():
        dn_s[...] = jnp.zeros_like(dn_s)
        hg_s[...] = jnp.zeros_like(hg_s)
        ml_c[...] = jnp.zeros_like(ml_c)
        ml_m[...] = jnp.full(ml_m.shape, NEG_BIG, _F32)

    n_sub = sf.shape[1] // CHUNK

    def chunk_step(j, carry):
        rows = (pl.ds(pl.multiple_of(j * CHUNK, CHUNK), CHUNK),
                pl.ds(pl.multiple_of((n_sub - 1 - j) * CHUNK, CHUNK), CHUNK))
        smt_f = sf[0, rows[0], :].T
        smt_b = sb[0, rows[1], :].T
        dn = _dn_body(rows, dn_af, smt_f, dn_ab, smt_b, dn_gp, dn_of, dn_ob, dn_s)
        hg = _hg_body(rows, hg_qf, hg_ff, hg_vf, hg_qb, hg_fb, hg_vb, hg_lb, hg_of, hg_ob, hg_s)
        ml = _ml_body(rows, ml_af, ml_vf, smt_f, ml_ab, ml_vb, smt_b, ml_gp, ml_of, ml_ob, ml_c, ml_m)
        pending = [dn, hg, dn, ml]
        while pending:
            for gen in list(pending):
                if gen in pending and next(gen, _DONE) is _DONE:
                    pending = [g for g in pending if g is not gen]
        return carry

    lax.fori_loop(0, n_sub, chunk_step, 0)


def _mix_call(zb3, zf3, dn_act, ml_act, dn_gates, hg_lb, ml_gates):
    b, s, _ = zb3.shape
    c = min(MIX_ROWS, s)
    n_chunks = s // c
    pair = lambda width, off: _pair_specs(c, n_chunks, width, off)
    sm, dn_a, ml_a = pair(HEAD_DIM, SM_OFF), pair(DN_ACT_W, 0), pair(ML_ACT_W, 0)
    hg_q, hg_v, ml_v = pair(MIX_WIDTH, HG_Q_OFF), pair(MIX_WIDTH, HG_I_OFF), pair(MIX_WIDTH, ML_V_OFF)
    hg_ff = pair(MIX_WIDTH, HG_F_OFF)[0]
    hg_fb = pair(MIX_WIDTH, HG_F_OFF + MIX_WIDTH)[1]
    in_specs = [sm[0], sm[1], dn_a[0], dn_a[1], _const_spec(dn_gates.shape),
                hg_q[0], hg_ff, hg_v[0], hg_q[1], hg_fb, hg_v[1], _const_spec(hg_lb.shape),
                ml_a[0], ml_v[0], ml_a[1], ml_v[1], _const_spec(ml_gates.shape)]
    args = (zf3, zf3, dn_act, dn_act, dn_gates, zb3, zf3, zb3, zb3, zf3, zb3, hg_lb,
            ml_act, zb3, ml_act, zb3, ml_gates)
    out = jax.ShapeDtypeStruct((b, s, MIX_WIDTH), _BF16)
    state = (2, N_HEADS, HEAD_DIM, HEAD_DIM)
    return pl.pallas_call(
        _mix_kernel,
        grid=(b, n_chunks),
        in_specs=in_specs,
        out_specs=pair(MIX_WIDTH, 0) * 3,
        out_shape=[out] * 6,
        scratch_shapes=[pltpu.VMEM(state, _F32), pltpu.VMEM(state, _F32),
                        pltpu.VMEM((2, N_HEADS, HEAD_DIM, 2 * HEAD_DIM), _F32),
                        pltpu.VMEM((2 * N_HEADS, HEAD_DIM), _F32)],
        compiler_params=pltpu.CompilerParams(dimension_semantics=("arbitrary", "arbitrary"),
                                             vmem_limit_bytes=VMEM_LIMIT),
        name="mixers",
    )(*args)


def _head_norm(o, w):
    parts = []
    for h in range(N_HEADS):
        seg = _head(o, h)
        parts.append(seg * lax.rsqrt(jnp.mean(seg * seg, axis=-1, keepdims=True) + EPS))
    return jnp.concatenate(parts, axis=1) * w


def _merge_kernel(x_ref, dnf, dnb, hgf, hgb, mlf, mlb, dnz, hgz, mlo, gp_ref,
                  nw_ref, wbr_ref, wout_ref, gpost_ref, o_ref):
    both = lambda f, b: f[...].astype(_F32) + b[...].astype(_F32)
    branches = (
        _head_norm(both(dnf, dnb), nw_ref[0:1, :]) * _silu_of_half(dnz[...].astype(_F32)),
        _head_norm(both(hgf, hgb), nw_ref[1:2, :]) * _silu_of_half(hgz[...].astype(_F32)),
        _head_norm(both(mlf, mlb), nw_ref[2:3, :]) * _two_sigmoid_of_half(mlo[...].astype(_F32)),
    )
    merged = None
    for i, br in enumerate(branches):
        gate = _two_sigmoid_of_half(gp_ref[:, i * D_MODEL:(i + 1) * D_MODEL].astype(_F32))
        term = gate * _bdot(br, wbr_ref[i])
        merged = term if merged is None else merged + term
    y = _bdot(merged, wout_ref[...])
    o_ref[...] = x_ref[...] + _rmsnorm(y, gpost_ref[...])


def _merge_call(x2, zb2, mix_outs, norm_w, w_branch, w_out, g_post):
    t = x2.shape[0]
    tm = min(MERGE_TM, t)
    row = lambda w: pl.BlockSpec((tm, w), lambda i: (i, 0))
    zblk = lambda w, off: pl.BlockSpec((tm, w), lambda i: (i, off // w))
    full = lambda a: pl.BlockSpec(a.shape, lambda i: (0,) * a.ndim)
    in_specs = ([row(D_MODEL)] + [row(MIX_WIDTH)] * 6
                + [zblk(MIX_WIDTH, DN_Z_OFF), zblk(MIX_WIDTH, HG_Z_OFF), zblk(MIX_WIDTH, ML_O_OFF),
                   zblk(N_BRANCH * D_MODEL, GP_OFF)]
                + [full(norm_w), full(w_branch), full(w_out), full(g_post)])
    return pl.pallas_call(
        _merge_kernel,
        grid=(t // tm,),
        in_specs=in_specs,
        out_specs=row(D_MODEL),
        out_shape=jax.ShapeDtypeStruct((t, D_MODEL), _F32),
        compiler_params=pltpu.CompilerParams(dimension_semantics=("arbitrary",),
                                             vmem_limit_bytes=VMEM_LIMIT),
        name="merge",
    )(x2, *mix_outs, zb2, zb2, zb2, zb2, norm_w, w_branch, w_out, g_post)


def _ffn_kernel(x_ref, gpre_ref, w1_ref, w2_ref, gpost_ref, o_ref):
    x = x_ref[...]
    u = jnp.maximum(_bdot(_rmsnorm(x, gpre_ref[...]), w1_ref[...]), 0.0)
    y = _bdot(u * u, w2_ref[...])
    o_ref[...] = x + _rmsnorm(y, gpost_ref[...])


def _ffn_call(x2, g_pre, w1, w2, g_post):
    t = x2.shape[0]
    tm = min(FFN_TM, t)
    row = pl.BlockSpec((tm, D_MODEL), lambda i: (i, 0))
    full = lambda a: pl.BlockSpec(a.shape, lambda i: (0,) * a.ndim, pipeline_mode=pl.Buffered(1))
    return pl.pallas_call(
        _ffn_kernel,
        grid=(t // tm,),
        in_specs=[row, full(g_pre), full(w1), full(w2), full(g_post)],
        out_specs=row,
        out_shape=jax.ShapeDtypeStruct((t, D_MODEL), _F32),
        compiler_params=pltpu.CompilerParams(dimension_semantics=("arbitrary",),
                                             vmem_limit_bytes=VMEM_LIMIT),
        name="ffn",
    )(x2, g_pre, w1, w2, g_post)


def _permute_w_in(w_in):
    m = MIX_WIDTH
    o_dn, o_sm_dn, o_hg, o_ml, o_sm_ml, o_gp = 0, 4 * m, 4 * m + 16, 9 * m + 16, 13 * m + 16, 13 * m + 32
    wide = [
        0.5 * w_in[:, o_gp:o_gp + N_BRANCH * D_MODEL],
        w_in[:, o_dn:o_dn + 3 * m],
        0.5 * w_in[:, o_dn + 3 * m:o_dn + 4 * m],
        w_in[:, o_ml:o_ml + 3 * m],
        0.5 * w_in[:, o_ml + 3 * m:o_ml + 4 * m],
        0.5 * w_in[:, o_hg:o_hg + m],
        w_in[:, o_hg + 3 * m:o_hg + 4 * m],
        0.5 * w_in[:, o_hg + 4 * m:o_hg + 5 * m],
        jnp.zeros((w_in.shape[0], NB_COLS - HG_Z_OFF - m), w_in.dtype),
    ]
    gates = [
        0.5 * w_in[:, o_hg + m:o_hg + 3 * m],
        w_in[:, o_sm_dn:o_sm_dn + 16],
        w_in[:, o_sm_ml:o_sm_ml + 16],
        jnp.zeros((w_in.shape[0], NF_COLS - SM_OFF - 32), w_in.dtype),
    ]
    return jnp.concatenate(wide, axis=1).astype(_BF16), jnp.concatenate(gates, axis=1).astype(_BF16)


def _gate_rows(first, second):
    vals = jnp.concatenate([first.astype(_F32).reshape(-1), second.astype(_F32).reshape(-1)])
    return jnp.broadcast_to(vals[:, None], (vals.shape[0], HEAD_DIM))


def _layer(x3, p):
    b, s, _ = x3.shape
    x2 = x3.reshape(b * s, D_MODEL)
    w_wide, w_gates = p["w_in"]
    zb2, zf2 = _proj(x2, p["norm_mix_pre"], w_wide, w_gates)
    zb3 = zb2.reshape(b, s, NB_COLS)
    zf3 = zf2.reshape(b, s, NF_COLS)
    dn_act, ml_act = _prep_call(zb3, p["dn_conv"], p["ml_conv"])
    mix = _mix_call(zb3, zf3, dn_act, ml_act, p["dn_gates"], p["hg_lb"], p["ml_gates"])
    mix = [a.reshape(b * s, MIX_WIDTH) for a in mix]
    x2 = _merge_call(x2, zb2, mix, p["mix_norm"], p["w_branch"], p["w_out"], p["norm_mix_post"])
    x2 = _ffn_call(x2, p["norm_ffn_pre"], p["w_ff1"], p["w_ff2"], p["norm_ffn_post"])
    return x2.reshape(b, s, D_MODEL)


def _layer_params(l, lower_bounds, norm_mix_pre, norm_mix_post, norm_ffn_pre, norm_ffn_post, w_in, dn_conv,
                  dn_a_log, dn_dt_bias, dn_norm, hg_norm, ml_conv, ml_i_bias, ml_f_bias, ml_norm,
                  w_branch, w_out, w_ff1, w_ff2):
    return {
        "norm_mix_pre": norm_mix_pre[l].reshape(1, D_MODEL),
        "norm_mix_post": norm_mix_post[l].reshape(1, D_MODEL),
        "norm_ffn_pre": norm_ffn_pre[l].reshape(1, D_MODEL),
        "norm_ffn_post": norm_ffn_post[l].reshape(1, D_MODEL),
        "w_in": _permute_w_in(w_in[l]),
        "dn_conv": 0.5 * dn_conv[l],
        "dn_gates": _gate_rows(dn_dt_bias[l], dn_a_log[l]),
        "hg_lb": lower_bounds[l],
        "ml_conv": 0.5 * ml_conv[l],
        "ml_gates": _gate_rows(ml_i_bias[l], ml_f_bias[l]),
        "mix_norm": jnp.stack([dn_norm[l].reshape(-1), hg_norm[l].reshape(-1), 0.5 * ml_norm[l].reshape(-1)]
                              + [jnp.zeros((MIX_WIDTH,), _F32)] * 5),
        "w_branch": (0.5 * w_branch[l]).astype(_BF16),
        "w_out": w_out[l].astype(_BF16),
        "w_ff1": w_ff1[l].astype(_BF16),
        "w_ff2": w_ff2[l].astype(_BF16),
    }


def kernel(x_prompt, x_sample, norm_mix_pre, norm_mix_post, norm_ffn_pre, norm_ffn_post, w_in, dn_conv,
           dn_a_log, dn_dt_bias, dn_norm, hg_lb, hg_norm, ml_conv, ml_i_bias, ml_f_bias, ml_norm,
           w_branch, w_out, w_ff1, w_ff2):
    p_layers = jax.nn.softmax(hg_lb.astype(_F32), axis=0)
    lower_bounds = jnp.cumsum(p_layers, axis=0) - p_layers[0]
    y_prompt, y_sample = x_prompt, x_sample
    for l in range(w_in.shape[0]):
        p = _layer_params(l, lower_bounds, norm_mix_pre, norm_mix_post, norm_ffn_pre, norm_ffn_post, w_in,
                          dn_conv, dn_a_log, dn_dt_bias, dn_norm, hg_norm, ml_conv, ml_i_bias, ml_f_bias,
                          ml_norm, w_branch, w_out, w_ff1, w_ff2)
        y_prompt = _layer(y_prompt, p)
        y_sample = _layer(y_sample, p)
    return (y_prompt, y_sample)
```

```python
import jax
import jax.numpy as jnp
from jax import lax
from jax.experimental import pallas as pl
from jax.experimental.pallas import tpu as pltpu

D_MODEL = 1024
N_HEADS = 4
HEAD_DIM = 128
MIX_WIDTH = N_HEADS * HEAD_DIM
N_BRANCH = 3
CONV_K = 5
CONV_HALO = 8
HALO_ROWS = 16
D_FF = 4 * D_MODEL
EPS = 1e-6
LOG2_E = 1.4426950408889634
NEG_BIG = -1e30
CHUNK = 128
MIX_ROWS = 4 * CHUNK
SUB = 4
SOLVE_BASE = 16

GP_OFF = 0
DN_QKV_OFF = 3072
DN_Z_OFF = 4608
ML_QK_OFF = 5120
ML_V_OFF = 6144
ML_O_OFF = 6656
HG_Q_OFF = 7168
HG_I_OFF = 7680
HG_Z_OFF = 8192
NB_COLS = 8960
HG_F_OFF = 0
SM_OFF = 1024
NF_COLS = 1152
SM_BETA, SM_ALPHA, SM_IG, SM_FG = 0, 8, 16, 24
DN_ACT_W = 3 * MIX_WIDTH
ML_ACT_W = 2 * MIX_WIDTH

PROJ_TM = 512
PREP_TM = 256
MERGE_TM = 512
FFN_TM = 512
VMEM_LIMIT = 56 * 1024 * 1024

_F32 = jnp.float32
_BF16 = jnp.bfloat16
_CHAINS = tuple((d, h) for d in range(2) for h in range(N_HEADS))
_DONE = object()


def _bdot(a, b):
    return jnp.dot(a.astype(_BF16), b.astype(_BF16), preferred_element_type=_F32)


def _bdot_nt(a, b):
    return lax.dot_general(a.astype(_BF16), b.astype(_BF16), (((1,), (1,)), ((), ())),
                           preferred_element_type=_F32)


def _bdot_tn(a, b):
    return lax.dot_general(a.astype(_BF16), b.astype(_BF16), (((0,), (0,)), ((), ())),
                           preferred_element_type=_F32)


def _sigmoid(x):
    return 0.5 * jnp.tanh(0.5 * x) + 0.5


def _silu_of_half(h):
    return h * (jnp.tanh(h) + 1.0)


def _two_sigmoid_of_half(h):
    return jnp.tanh(h) + 1.0


def _softplus(x):
    return jnp.maximum(x, 0.0) + jnp.log(1.0 + jnp.exp(-jnp.abs(x)))


def _rmsnorm(x, g):
    return x * lax.rsqrt(jnp.mean(x * x, axis=-1, keepdims=True) + EPS) * g


def _split3(x):
    mask = jnp.uint32(0xFFFF0000)

    def top(v):
        return lax.bitcast_convert_type(lax.bitcast_convert_type(v, jnp.uint32) & mask, _F32)

    hi = top(x)
    r1 = x - hi
    mid = top(r1)
    return hi.astype(_BF16), mid.astype(_BF16), (r1 - mid).astype(_BF16)


def _cumsum_rows(x, rev):
    c, w = x.shape
    row = lax.broadcasted_iota(jnp.int32, (c, c), 0)
    col = lax.broadcasted_iota(jnp.int32, (c, c), 1)
    tri = jnp.where((col >= row) if rev else (col <= row), 1.0, 0.0).astype(_BF16)
    s = jnp.dot(tri, jnp.concatenate(_split3(x), axis=1), preferred_element_type=_F32)
    return s[:, :w] + s[:, w:2 * w] + s[:, 2 * w:]


def _cumsum_lanes(x, rev):
    r, c = x.shape
    row = lax.broadcasted_iota(jnp.int32, (c, c), 0)
    col = lax.broadcasted_iota(jnp.int32, (c, c), 1)
    tri = jnp.where((row >= col) if rev else (row <= col), 1.0, 0.0).astype(_BF16)
    s = jnp.dot(jnp.concatenate(_split3(x), axis=0), tri, preferred_element_type=_F32)
    return s[:r] + s[r:2 * r] + s[2 * r:]


def _head(t, h):
    return t[:, h * HEAD_DIM:(h + 1) * HEAD_DIM]


def _proj_kernel(x_ref, g_ref, wb_ref, wf_ref, ob_ref, of_ref):
    h = _rmsnorm(x_ref[...], g_ref[...]).astype(_BF16)
    ob_ref[...] = jnp.dot(h, wb_ref[...], preferred_element_type=_F32).astype(_BF16)
    of_ref[...] = jnp.dot(h, wf_ref[...], preferred_element_type=_F32)


def _proj(x2, g, w_wide, w_gates):
    t = x2.shape[0]
    tm = min(PROJ_TM, t)
    resident = lambda a: pl.BlockSpec(a.shape, lambda i: (0, 0), pipeline_mode=pl.Buffered(1))
    return pl.pallas_call(
        _proj_kernel,
        grid=(t // tm,),
        in_specs=[pl.BlockSpec((tm, D_MODEL), lambda i: (i, 0)), resident(g), resident(w_wide), resident(w_gates)],
        out_specs=[pl.BlockSpec((tm, NB_COLS), lambda i: (i, 0)), pl.BlockSpec((tm, NF_COLS), lambda i: (i, 0))],
        out_shape=[jax.ShapeDtypeStruct((t, NB_COLS), _BF16), jax.ShapeDtypeStruct((t, NF_COLS), _F32)],
        compiler_params=pltpu.CompilerParams(dimension_semantics=("arbitrary",), vmem_limit_bytes=VMEM_LIMIT),
        name="proj",
    )(x2, g, w_wide, w_gates)


def _conv_act(prev8, cur, next8, w, first, last):
    c = cur.shape[0]
    prev8 = jnp.where(first, 0.0, prev8[HALO_ROWS - CONV_HALO:].astype(_F32))
    next8 = jnp.where(last, 0.0, next8[:CONV_HALO].astype(_F32))
    ext = jnp.concatenate([prev8, cur.astype(_F32), next8], axis=0)
    rows = c + 2 * CONV_HALO
    y = None
    for j in range(CONV_K):
        sh = (CONV_K // 2 - j) % rows
        tap = ext if sh == 0 else pltpu.roll(ext, sh, 0)
        term = w[j:j + 1] * tap[CONV_HALO:CONV_HALO + c]
        y = term if y is None else y + term
    return _silu_of_half(y)


def _prep_kernel(dq, dp, dn, mq, mp, mn, dw_ref, mw_ref, dn_out, ml_out):
    n = pl.program_id(1)
    first = n == 0
    last = n == pl.num_programs(1) - 1
    for h in range(3 * N_HEADS):
        sl = slice(h * HEAD_DIM, (h + 1) * HEAD_DIM)
        t = _conv_act(dp[0, :, sl], dq[0, :, sl], dn[0, :, sl], dw_ref[:, sl], first, last)
        if h < 2 * N_HEADS:
            inv = lax.rsqrt(jnp.sum(t * t, axis=-1, keepdims=True) + EPS)
            t = t * (inv * (HEAD_DIM ** -0.5) if h < N_HEADS else inv)
        dn_out[0, :, sl] = t.astype(_BF16)
    for h in range(2 * N_HEADS):
        sl = slice(h * HEAD_DIM, (h + 1) * HEAD_DIM)
        t = _conv_act(mp[0, :, sl], mq[0, :, sl], mn[0, :, sl], mw_ref[:, sl], first, last)
        ml_out[0, :, sl] = (t * (HEAD_DIM ** -0.5) if h < N_HEADS else t).astype(_BF16)


def _halo_specs(tm, seq, width, off):
    blk = off // width
    r8 = tm // HALO_ROWS
    last8 = seq // HALO_ROWS - 1
    return [pl.BlockSpec((1, tm, width), lambda b, n: (b, n, blk)),
            pl.BlockSpec((1, HALO_ROWS, width), lambda b, n: (b, jnp.maximum(n * r8 - 1, 0), blk)),
            pl.BlockSpec((1, HALO_ROWS, width), lambda b, n: (b, jnp.minimum((n + 1) * r8, last8), blk))]


def _prep_call(zb3, dn_conv, ml_conv):
    b, s, _ = zb3.shape
    tm = min(PREP_TM, s)
    in_specs = (_halo_specs(tm, s, DN_ACT_W, DN_QKV_OFF) + _halo_specs(tm, s, ML_ACT_W, ML_QK_OFF)
                + [_const_spec(dn_conv.shape), _const_spec(ml_conv.shape)])
    return pl.pallas_call(
        _prep_kernel,
        grid=(b, s // tm),
        in_specs=in_specs,
        out_specs=[pl.BlockSpec((1, tm, DN_ACT_W), lambda b, n: (b, n, 0)),
                   pl.BlockSpec((1, tm, ML_ACT_W), lambda b, n: (b, n, 0))],
        out_shape=[jax.ShapeDtypeStruct((b, s, DN_ACT_W), _BF16),
                   jax.ShapeDtypeStruct((b, s, ML_ACT_W), _BF16)],
        compiler_params=pltpu.CompilerParams(dimension_semantics=("arbitrary", "arbitrary"),
                                             vmem_limit_bytes=VMEM_LIMIT),
        name="prep",
    )(zb3, zb3, zb3, zb3, zb3, zb3, dn_conv, ml_conv)


def _pair_specs(c, n_chunks, width, off):
    blk = off // width
    return [pl.BlockSpec((1, c, width), lambda b, n: (b, n, blk)),
            pl.BlockSpec((1, c, width), lambda b, n: (b, n_chunks - 1 - n, blk))]


def _const_spec(shape):
    nd = len(shape)
    return pl.BlockSpec(shape, lambda b, n: (0,) * nd)


def _tri_masks(c, rev):
    row = lax.broadcasted_iota(jnp.int32, (c, c), 0)
    col = lax.broadcasted_iota(jnp.int32, (c, c), 1)
    return ((col >= row), (col > row)) if rev else ((col <= row), (col < row))


def _unit_tri_inverse_many(mats):
    c = mats[0].shape[0]
    row = lax.broadcasted_iota(jnp.int32, (c, c), 0)
    col = lax.broadcasted_iota(jnp.int32, (c, c), 1)
    blk = min(SOLVE_BASE, c)
    eye = jnp.where(row == col, 1.0, 0.0).astype(_F32)
    diag_blk = (row // blk) == (col // blk)
    a0 = [jnp.where(diag_blk, a, 0.0) for a in mats]
    ts = [eye - x for x in a0]
    pws = [_bdot(x, x) for x in a0]
    yield
    k = 2
    while 2 * k < blk:
        boths = [_bdot(jnp.concatenate([t, pw], axis=0), pw) for t, pw in zip(ts, pws)]
        ts = [t + both[:c] for t, both in zip(ts, boths)]
        pws = [both[c:] for both in boths]
        k *= 2
        yield
    ts = [t + _bdot(t, pw) for t, pw in zip(ts, pws)]
    yield
    while blk < c:
        off = ((row // (2 * blk)) == (col // (2 * blk))) & ((row // blk) != (col // blk))
        tl = [_bdot(t, jnp.where(off, a, 0.0)) for t, a in zip(ts, mats)]
        yield
        ts = [t - _bdot(x, t) for t, x in zip(ts, tl)]
        yield
        blk *= 2
    return ts


def _dn_body(rows, af, ab, gates_in, of_ref, ob_ref, s_ref):
    c = CHUNK
    n_g = 2 * N_HEADS
    gates = [dict(cols=cols, gcum=grow[0:n_g], e_tot=grow[n_g:2 * n_g, 0:1], masks=_tri_masks(c, rev))
             for rev, (cols, grow) in zip((False, True), gates_in)]
    acts = (af, ab)
    q, k, kb, vb, eg, a_low, attn = [], [], [], [], [], [], []
    for d, h in _CHAINS:
        g = gates[d]
        r = d * N_HEADS + h
        cols = g["cols"]
        incl, strict = g["masks"]
        qh = acts[d][0, rows[d], h * HEAD_DIM:(h + 1) * HEAD_DIM]
        kh = acts[d][0, rows[d], MIX_WIDTH + h * HEAD_DIM:MIX_WIDTH + (h + 1) * HEAD_DIM]
        vh = acts[d][0, rows[d], 2 * MIX_WIDTH + h * HEAD_DIM:2 * MIX_WIDTH + (h + 1) * HEAD_DIM]
        beta = cols[:, r:r + 1]
        decay = jnp.exp(jnp.where(incl, cols[:, n_g + r:n_g + r + 1] - g["gcum"][r:r + 1, :], NEG_BIG))
        kbh = kh.astype(_F32) * beta
        both = _bdot_nt(jnp.concatenate([kbh.astype(_BF16), qh], axis=0), kh)
        a_low.append(jnp.where(strict, both[:c] * decay, 0.0))
        attn.append(both[c:] * decay)
        q.append(qh)
        k.append(kh)
        kb.append(kbh)
        vb.append(vh.astype(_F32) * beta)
        eg.append(cols[:, 2 * n_g + r:2 * n_g + r + 1])
        if h == N_HEADS - 1:
            yield
    tinv = yield from _unit_tri_inverse_many(a_low)
    sol = [_bdot(t, jnp.concatenate([vb[i], kb[i] * eg[i]], axis=1)) for i, t in enumerate(tinv)]
    yield
    ws_qs = [_bdot(jnp.concatenate([sol[i][:, HEAD_DIM:], q[i].astype(_F32) * eg[i]], axis=0), s_ref[d, h])
             for i, (d, h) in enumerate(_CHAINS)]
    yield
    v_new = [sol[i][:, :HEAD_DIM] - ws_qs[i][:c] for i in range(len(_CHAINS))]
    outs = [ws_qs[i][c:] + _bdot(attn[i], v_new[i]) for i in range(len(_CHAINS))]
    yield
    for i, (d, h) in enumerate(_CHAINS):
        r = d * N_HEADS + h
        er = gates[d]["cols"][:, 3 * n_g + r:3 * n_g + r + 1]
        et = gates[d]["e_tot"][r:r + 1, :]
        s_ref[d, h] = s_ref[d, h] * et + _bdot_tn(k[i].astype(_F32) * er, v_new[i])
    of_ref[0, rows[0], :] = jnp.concatenate(outs[:N_HEADS], axis=1).astype(_BF16)
    ob_ref[0, rows[1], :] = jnp.concatenate(outs[N_HEADS:], axis=1).astype(_BF16)


def _hg_level_masks(c, rev):
    row = lax.broadcasted_iota(jnp.int32, (c, c), 0)
    col = lax.broadcasted_iota(jnp.int32, (c, c), 1)
    levels = []
    blk = c // 2
    while blk >= SUB:
        same_pair = (row // (2 * blk)) == (col // (2 * blk))
        row_hi = (row // blk) % 2 == 1
        col_hi = (col // blk) % 2 == 1
        mask = (same_pair & (~row_hi) & col_hi) if rev else (same_pair & row_hi & (~col_hi))
        levels.append((blk, mask))
        blk //= 2
    return levels


def _hg_band_masks(c, rev):
    row = lax.broadcasted_iota(jnp.int32, (c, c), 0)
    col = lax.broadcasted_iota(jnp.int32, (c, c), 1)
    masks = []
    for dl in range(SUB):
        if rev:
            masks.append((col == row + dl) & ((row % SUB) + dl < SUB))
        else:
            masks.append((col == row - dl) & ((row % SUB) >= dl))
    return masks


def _group_roll(x, shift):
    c, w = x.shape
    return pltpu.roll(x.reshape(c // 8, 8, w), shift % 8, 1).reshape(c, w)


def _pair_reference(b, blk, rev):
    c, w = b.shape
    b3 = b.reshape(c // (2 * blk), 2 * blk, w)
    ref = b3[:, blk:blk + 1, :] if rev else b3[:, blk - 1:blk, :]
    return jnp.broadcast_to(ref, b3.shape).reshape(c, w)


def _hg_body(rows, qf, ff, vf, qb, fb, vb, lb_ref, of_ref, ob_ref, s_ref):
    c = CHUNK
    dirs = []
    for d, (q_ref, f_ref, v_ref) in enumerate(((qf, ff, vf), (qb, fb, vb))):
        rev = d == 1
        lb = lb_ref[d:d + 1, :]
        half_span = 0.5 * (1.0 - lb)
        t_f = jnp.tanh(f_ref[0, rows[d], :]) * half_span
        q_all = _silu_of_half(q_ref[0, rows[d], :].astype(_F32))
        k_all = half_span - t_f
        f_all = (lb + half_span) + t_f
        logf = jnp.log(f_all)
        b_all = _cumsum_rows(logf, rev)
        btot = jnp.sum(logf, axis=0, keepdims=True)
        dirs.append(dict(rev=rev, q=q_all, k=k_all, f=f_all, v=v_ref[0, rows[d], :],
                         b2=b_all * LOG2_E,
                         qe=q_all * jnp.exp(b_all), ke=k_all * jnp.exp(btot - b_all), e_tot=jnp.exp(btot),
                         levels=_hg_level_masks(c, rev), bands=_hg_band_masks(c, rev)))
        yield
    lvl_ops, band_sums = [], []
    for d, h in _CHAINS:
        g = dirs[d]
        rev = g["rev"]
        q = _head(g["q"], h)
        k = _head(g["k"], h)
        f = _head(g["f"], h)
        b2 = _head(g["b2"], h)
        q16 = q.astype(_BF16)
        k16 = k.astype(_BF16)
        ops = []
        for blk, _ in g["levels"]:
            e = jnp.exp2(-jnp.abs(b2 - _pair_reference(b2, blk, rev))).astype(_BF16)
            ops.append((q16 * e, k16 * e))
        lvl_ops.append(ops)
        sums = []
        qd = q
        for dl in range(SUB):
            if dl > 0:
                qd = qd * (f if dl == 1 else _group_roll(f, (1 - dl) if rev else (dl - 1)))
            k_dl = k if dl == 0 else _group_roll(k, -dl if rev else dl)
            sums.append(jnp.sum(qd * k_dl, axis=-1, keepdims=True))
        band_sums.append(sums)
        if h % 2 == 1:
            yield
    prods = []
    for i in range(len(_CHAINS)):
        prods.append([_bdot_nt(qs, ks) for qs, ks in lvl_ops[i]])
        if i % 2 == 1:
            yield
    attn = []
    for i, (d, h) in enumerate(_CHAINS):
        g = dirs[d]
        a = jnp.zeros((c, c), _F32)
        for (_, mask), prod in zip(g["levels"], prods[i]):
            a = jnp.where(mask, prod, a)
        for dl in range(SUB):
            a = jnp.where(g["bands"][dl], band_sums[i][dl], a)
        attn.append(a)
    yield
    outs = []
    for i, (d, h) in enumerate(_CHAINS):
        g = dirs[d]
        state_t = s_ref[d, h]
        outs.append(_bdot_nt(_head(g["qe"], h), state_t) + _bdot(attn[i], _head(g["v"], h)))
        s_ref[d, h] = state_t * _head(g["e_tot"], h) + _bdot_tn(_head(g["v"], h), _head(g["ke"], h))
        if h % 2 == 1:
            yield
    of_ref[0, rows[0], :] = jnp.concatenate(outs[:N_HEADS], axis=1).astype(_BF16)
    ob_ref[0, rows[1], :] = jnp.concatenate(outs[N_HEADS:], axis=1).astype(_BF16)


def _ml_body(rows, af, vf, ab, vb, gates_in, of_ref, ob_ref, c_ref, m_ref):
    c = CHUNK
    n_g = 2 * N_HEADS
    lane = lax.broadcasted_iota(jnp.int32, (c, HEAD_DIM), 1)
    ones_col = jnp.where(lane == 0, 1.0, 0.0).astype(_BF16)
    gates = [dict(cols=cols, ig=grow[0:n_g], bcum=grow[n_g:2 * n_g], btot=grow[2 * n_g:3 * n_g, 0:1],
                  incl=_tri_masks(c, rev)[0])
             for rev, (cols, grow) in zip((False, True), gates_in)]
    acts = (af, ab)
    vals = (vf, vb)
    q, k, v_ext, qk, qs = [], [], [], [], []
    for d, h in _CHAINS:
        qh = acts[d][0, rows[d], h * HEAD_DIM:(h + 1) * HEAD_DIM]
        kh = acts[d][0, rows[d], MIX_WIDTH + h * HEAD_DIM:MIX_WIDTH + (h + 1) * HEAD_DIM]
        vh = vals[d][0, rows[d], h * HEAD_DIM:(h + 1) * HEAD_DIM]
        q.append(qh)
        k.append(kh)
        v_ext.append(jnp.concatenate([vh.astype(_BF16), ones_col], axis=1))
        qk.append(_bdot_nt(qh, kh))
        qs.append(_bdot(qh, c_ref[d, h]))
        if h == N_HEADS - 1:
            yield
    n_ch = len(_CHAINS)
    logw, inter, logw_end, bt_m = [], [], [], []
    for d, h in _CHAINS:
        g = gates[d]
        col = d * N_HEADS + h
        b_col = g["cols"][:, n_g + col:n_g + col + 1]
        i_col = g["cols"][:, col:col + 1]
        b_row = g["bcum"][col:col + 1, :]
        i_row = g["ig"][col:col + 1, :]
        bt = g["btot"][col:col + 1, :]
        m_old = m_ref[col:col + 1, 0:1]
        logw.append(jnp.where(g["incl"], b_col - b_row + i_row, NEG_BIG))
        inter.append(b_col + m_old)
        logw_end.append(bt - b_col + i_col)
        bt_m.append(bt + m_old)
    yield
    m_row = [jnp.maximum(inter[i], jnp.max(logw[i], axis=-1, keepdims=True)) for i in range(n_ch)]
    m_new = [jnp.maximum(bt_m[i], jnp.max(logw_end[i], axis=0, keepdims=True)) for i in range(n_ch)]
    yield
    p = [qk[i] * jnp.exp(logw[i] - m_row[i]) for i in range(n_ch)]
    yield
    pv = [_bdot(p[i], v_ext[i]) for i in range(n_ch)]
    yield
    outs = []
    for i in range(n_ch):
        nd = jnp.exp(inter[i] - m_row[i]) * qs[i] + pv[i]
        den = nd[:, HEAD_DIM:HEAD_DIM + 1]
        outs.append(nd[:, :HEAD_DIM] / jnp.maximum(jnp.abs(den), jnp.exp(-m_row[i])))
    yield
    kw = [k[i].astype(_F32) * jnp.exp(logw_end[i] - m_new[i]) for i in range(n_ch)]
    for i, (d, h) in enumerate(_CHAINS):
        col = d * N_HEADS + h
        c_ref[d, h] = jnp.exp(bt_m[i] - m_new[i]) * c_ref[d, h] + _bdot_tn(kw[i], v_ext[i])
        m_ref[col:col + 1, :] = jnp.broadcast_to(m_new[i], (1, HEAD_DIM))
    of_ref[0, rows[0], :] = jnp.concatenate(outs[:N_HEADS], axis=1).astype(_BF16)
    ob_ref[0, rows[1], :] = jnp.concatenate(outs[N_HEADS:], axis=1).astype(_BF16)


def _scalar_gates(sf, sb, n_sub, dn_gp, ml_gp, dn_gc, dn_gr, ml_gc, ml_gr):
    c = CHUNK
    n_g = 2 * N_HEADS
    tiles = [(d, jb) for d in range(2) for jb in range(n_sub)]
    smts = [(sf, sb)[d][0, jb * c:(jb + 1) * c, :].T for d, jb in tiles]
    betas = [_sigmoid(t[SM_BETA:SM_BETA + n_g]) for t in smts]
    gs = [-jnp.exp(dn_gp[n_g:2 * n_g]) * _softplus(t[SM_ALPHA:SM_ALPHA + n_g] + dn_gp[0:n_g]) for t in smts]
    igs = [t[SM_IG:SM_IG + n_g] + ml_gp[0:n_g] for t in smts]
    fts = [t[SM_FG:SM_FG + n_g] + ml_gp[n_g:2 * n_g] for t in smts]
    logfs = [jnp.minimum(f, 0.0) - jnp.log(1.0 + jnp.exp(-jnp.abs(f))) for f in fts]
    cums = [_cumsum_lanes(jnp.concatenate(gs[d * n_sub:(d + 1) * n_sub] + logfs[d * n_sub:(d + 1) * n_sub], axis=0),
                          d == 1) for d in range(2)]
    gcums = [cums[d][jb * n_g:(jb + 1) * n_g] for d, jb in tiles]
    bcums = [cums[d][(n_sub + jb) * n_g:(n_sub + jb + 1) * n_g] for d, jb in tiles]
    gtots = [jnp.sum(g, axis=1, keepdims=True) for g in gs]
    btots = [jnp.sum(f, axis=1, keepdims=True) for f in logfs]
    pad = lambda n: jnp.zeros((HEAD_DIM - n * n_g, c), _F32)
    dn_cols = [jnp.concatenate([betas[i], gcums[i], jnp.exp(gcums[i]), jnp.exp(gtots[i] - gcums[i]), pad(4)],
                               axis=0).T for i in range(len(tiles))]
    ml_cols = [jnp.concatenate([igs[i], bcums[i], pad(2)], axis=0).T for i in range(len(tiles))]
    for i, (d, jb) in enumerate(tiles):
        dn_gc[d, jb] = dn_cols[i]
        dn_gr[d, jb] = jnp.concatenate([gcums[i], jnp.broadcast_to(jnp.exp(gtots[i]), (n_g, c))], axis=0)
        ml_gc[d, jb] = ml_cols[i]
        ml_gr[d, jb] = jnp.concatenate([igs[i], bcums[i], jnp.broadcast_to(btots[i], (n_g, c))], axis=0)


def _mix_kernel(sf, sb, dn_af, dn_ab, dn_gp, hg_qf, hg_ff, hg_vf, hg_qb, hg_fb, hg_vb, hg_lb,
                ml_af, ml_vf, ml_ab, ml_vb, ml_gp,
                dn_of, dn_ob, hg_of, hg_ob, ml_of, ml_ob,
                dn_s, hg_s, ml_c, ml_m, dn_gc, dn_gr, ml_gc, ml_gr):
    @pl.when(pl.program_id(1) == 0)
    def _():
        dn_s[...] = jnp.zeros_like(dn_s)
        hg_s[...] = jnp.zeros_like(hg_s)
        ml_c[...] = jnp.zeros_like(ml_c)
        ml_m[...] = jnp.full(ml_m.shape, NEG_BIG, _F32)

    n_sub = sf.shape[1] // CHUNK
    _scalar_gates(sf, sb, n_sub, dn_gp, ml_gp, dn_gc, dn_gr, ml_gc, ml_gr)

    def chunk_step(j, carry):
        jb = (j, n_sub - 1 - j)
        rows = tuple(pl.ds(pl.multiple_of(i * CHUNK, CHUNK), CHUNK) for i in jb)
        dn = _dn_body(rows, dn_af, dn_ab, [(dn_gc[d, jb[d]], dn_gr[d, jb[d]]) for d in range(2)],
                      dn_of, dn_ob, dn_s)
        hg = _hg_body(rows, hg_qf, hg_ff, hg_vf, hg_qb, hg_fb, hg_vb, hg_lb, hg_of, hg_ob, hg_s)
        ml = _ml_body(rows, ml_af, ml_vf, ml_ab, ml_vb, [(ml_gc[d, jb[d]], ml_gr[d, jb[d]]) for d in range(2)],
                      ml_of, ml_ob, ml_c, ml_m)
        pending = [dn, hg, dn, ml]
        while pending:
            for gen in list(pending):
                if gen in pending and next(gen, _DONE) is _DONE:
                    pending = [g for g in pending if g is not gen]
        return carry

    lax.fori_loop(0, n_sub, chunk_step, 0)


def _mix_call(zb3, zf3, dn_act, ml_act, dn_gates, hg_lb, ml_gates):
    b, s, _ = zb3.shape
    c = min(MIX_ROWS, s)
    n_chunks = s // c
    n_sub = c // CHUNK
    pair = lambda width, off: _pair_specs(c, n_chunks, width, off)
    sm, dn_a, ml_a = pair(HEAD_DIM, SM_OFF), pair(DN_ACT_W, 0), pair(ML_ACT_W, 0)
    hg_q, hg_v, ml_v = pair(MIX_WIDTH, HG_Q_OFF), pair(MIX_WIDTH, HG_I_OFF), pair(MIX_WIDTH, ML_V_OFF)
    hg_ff = pair(MIX_WIDTH, HG_F_OFF)[0]
    hg_fb = pair(MIX_WIDTH, HG_F_OFF + MIX_WIDTH)[1]
    in_specs = [sm[0], sm[1], dn_a[0], dn_a[1], _const_spec(dn_gates.shape),
                hg_q[0], hg_ff, hg_v[0], hg_q[1], hg_fb, hg_v[1], _const_spec(hg_lb.shape),
                ml_a[0], ml_v[0], ml_a[1], ml_v[1], _const_spec(ml_gates.shape)]
    args = (zf3, zf3, dn_act, dn_act, dn_gates, zb3, zf3, zb3, zb3, zf3, zb3, hg_lb,
            ml_act, zb3, ml_act, zb3, ml_gates)
    out = jax.ShapeDtypeStruct((b, s, MIX_WIDTH), _BF16)
    state = (2, N_HEADS, HEAD_DIM, HEAD_DIM)
    return pl.pallas_call(
        _mix_kernel,
        grid=(b, n_chunks),
        in_specs=in_specs,
        out_specs=pair(MIX_WIDTH, 0) * 3,
        out_shape=[out] * 6,
        scratch_shapes=[pltpu.VMEM(state, _F32), pltpu.VMEM(state, _F32),
                        pltpu.VMEM((2, N_HEADS, HEAD_DIM, 2 * HEAD_DIM), _F32),
                        pltpu.VMEM((2 * N_HEADS, HEAD_DIM), _F32),
                        pltpu.VMEM((2, n_sub, CHUNK, HEAD_DIM), _F32), pltpu.VMEM((2, n_sub, 4 * N_HEADS, CHUNK), _F32),
                        pltpu.VMEM((2, n_sub, CHUNK, HEAD_DIM), _F32), pltpu.VMEM((2, n_sub, 6 * N_HEADS, CHUNK), _F32)],
        compiler_params=pltpu.CompilerParams(dimension_semantics=("arbitrary", "arbitrary"),
                                             vmem_limit_bytes=VMEM_LIMIT),
        name="mixers",
    )(*args)


def _head_norm(o, w):
    parts = []
    for h in range(N_HEADS):
        seg = _head(o, h)
        parts.append(seg * lax.rsqrt(jnp.mean(seg * seg, axis=-1, keepdims=True) + EPS))
    return jnp.concatenate(parts, axis=1) * w


def _merge_kernel(x_ref, dnf, dnb, hgf, hgb, mlf, mlb, dnz, hgz, mlo, gp_ref,
                  nw_ref, wbr_ref, wout_ref, gpost_ref, o_ref):
    both = lambda f, b: f[...].astype(_F32) + b[...].astype(_F32)
    branches = (
        _head_norm(both(dnf, dnb), nw_ref[0:1, :]) * _silu_of_half(dnz[...].astype(_F32)),
        _head_norm(both(hgf, hgb), nw_ref[1:2, :]) * _silu_of_half(hgz[...].astype(_F32)),
        _head_norm(both(mlf, mlb), nw_ref[2:3, :]) * _two_sigmoid_of_half(mlo[...].astype(_F32)),
    )
    merged = None
    for i, br in enumerate(branches):
        gate = _two_sigmoid_of_half(gp_ref[:, i * D_MODEL:(i + 1) * D_MODEL].astype(_F32))
        term = gate * _bdot(br, wbr_ref[i])
        merged = term if merged is None else merged + term
    y = _bdot(merged, wout_ref[...])
    o_ref[...] = x_ref[...] + _rmsnorm(y, gpost_ref[...])


def _merge_call(x2, zb2, mix_outs, norm_w, w_branch, w_out, g_post):
    t = x2.shape[0]
    tm = min(MERGE_TM, t)
    row = lambda w: pl.BlockSpec((tm, w), lambda i: (i, 0))
    zblk = lambda w, off: pl.BlockSpec((tm, w), lambda i: (i, off // w))
    full = lambda a: pl.BlockSpec(a.shape, lambda i: (0,) * a.ndim)
    in_specs = ([row(D_MODEL)] + [row(MIX_WIDTH)] * 6
                + [zblk(MIX_WIDTH, DN_Z_OFF), zblk(MIX_WIDTH, HG_Z_OFF), zblk(MIX_WIDTH, ML_O_OFF),
                   zblk(N_BRANCH * D_MODEL, GP_OFF)]
                + [full(norm_w), full(w_branch), full(w_out), full(g_post)])
    return pl.pallas_call(
        _merge_kernel,
        grid=(t // tm,),
        in_specs=in_specs,
        out_specs=row(D_MODEL),
        out_shape=jax.ShapeDtypeStruct((t, D_MODEL), _F32),
        compiler_params=pltpu.CompilerParams(dimension_semantics=("arbitrary",),
                                             vmem_limit_bytes=VMEM_LIMIT),
        name="merge",
    )(x2, *mix_outs, zb2, zb2, zb2, zb2, norm_w, w_branch, w_out, g_post)


def _ffn_kernel(x_ref, gpre_ref, w1_ref, w2_ref, gpost_ref, o_ref):
    x = x_ref[...]
    u = jnp.maximum(_bdot(_rmsnorm(x, gpre_ref[...]), w1_ref[...]), 0.0)
    y = _bdot(u * u, w2_ref[...])
    o_ref[...] = x + _rmsnorm(y, gpost_ref[...])


def _ffn_call(x2, g_pre, w1, w2, g_post):
    t = x2.shape[0]
    tm = min(FFN_TM, t)
    row = pl.BlockSpec((tm, D_MODEL), lambda i: (i, 0))
    full = lambda a: pl.BlockSpec(a.shape, lambda i: (0,) * a.ndim, pipeline_mode=pl.Buffered(1))
    return pl.pallas_call(
        _ffn_kernel,
        grid=(t // tm,),
        in_specs=[row, full(g_pre), full(w1), full(w2), full(g_post)],
        out_specs=row,
        out_shape=jax.ShapeDtypeStruct((t, D_MODEL), _F32),
        compiler_params=pltpu.CompilerParams(dimension_semantics=("arbitrary",),
                                             vmem_limit_bytes=VMEM_LIMIT),
        name="ffn",
    )(x2, g_pre, w1, w2, g_post)


def _permute_w_in(w_in):
    m = MIX_WIDTH
    o_dn, o_sm_dn, o_hg, o_ml, o_sm_ml, o_gp = 0, 4 * m, 4 * m + 16, 9 * m + 16, 13 * m + 16, 13 * m + 32
    wide = [
        0.5 * w_in[:, o_gp:o_gp + N_BRANCH * D_MODEL],
        w_in[:, o_dn:o_dn + 3 * m],
        0.5 * w_in[:, o_dn + 3 * m:o_dn + 4 * m],
        w_in[:, o_ml:o_ml + 3 * m],
        0.5 * w_in[:, o_ml + 3 * m:o_ml + 4 * m],
        0.5 * w_in[:, o_hg:o_hg + m],
        w_in[:, o_hg + 3 * m:o_hg + 4 * m],
        0.5 * w_in[:, o_hg + 4 * m:o_hg + 5 * m],
        jnp.zeros((w_in.shape[0], NB_COLS - HG_Z_OFF - m), w_in.dtype),
    ]
    gates = [
        0.5 * w_in[:, o_hg + m:o_hg + 3 * m],
        w_in[:, o_sm_dn:o_sm_dn + 16],
        w_in[:, o_sm_ml:o_sm_ml + 16],
        jnp.zeros((w_in.shape[0], NF_COLS - SM_OFF - 32), w_in.dtype),
    ]
    return jnp.concatenate(wide, axis=1).astype(_BF16), jnp.concatenate(gates, axis=1).astype(_BF16)


def _gate_rows(first, second):
    vals = jnp.concatenate([first.astype(_F32).reshape(-1), second.astype(_F32).reshape(-1)])
    return jnp.broadcast_to(vals[:, None], (vals.shape[0], HEAD_DIM))


def _layer(x3, p):
    b, s, _ = x3.shape
    x2 = x3.reshape(b * s, D_MODEL)
    w_wide, w_gates = p["w_in"]
    zb2, zf2 = _proj(x2, p["norm_mix_pre"], w_wide, w_gates)
    zb3 = zb2.reshape(b, s, NB_COLS)
    zf3 = zf2.reshape(b, s, NF_COLS)
    dn_act, ml_act = _prep_call(zb3, p["dn_conv"], p["ml_conv"])
    mix = _mix_call(zb3, zf3, dn_act, ml_act, p["dn_gates"], p["hg_lb"], p["ml_gates"])
    mix = [a.reshape(b * s, MIX_WIDTH) for a in mix]
    x2 = _merge_call(x2, zb2, mix, p["mix_norm"], p["w_branch"], p["w_out"], p["norm_mix_post"])
    x2 = _ffn_call(x2, p["norm_ffn_pre"], p["w_ff1"], p["w_ff2"], p["norm_ffn_post"])
    return x2.reshape(b, s, D_MODEL)


def _layer_params(l, lower_bounds, norm_mix_pre, norm_mix_post, norm_ffn_pre, norm_ffn_post, w_in, dn_conv,
                  dn_a_log, dn_dt_bias, dn_norm, hg_norm, ml_conv, ml_i_bias, ml_f_bias, ml_norm,
                  w_branch, w_out, w_ff1, w_ff2):
    return {
        "norm_mix_pre": norm_mix_pre[l].reshape(1, D_MODEL),
        "norm_mix_post": norm_mix_post[l].reshape(1, D_MODEL),
        "norm_ffn_pre": norm_ffn_pre[l].reshape(1, D_MODEL),
        "norm_ffn_post": norm_ffn_post[l].reshape(1, D_MODEL),
        "w_in": _permute_w_in(w_in[l]),
        "dn_conv": 0.5 * dn_conv[l],
        "dn_gates": _gate_rows(dn_dt_bias[l], dn_a_log[l]),
        "hg_lb": lower_bounds[l],
        "ml_conv": 0.5 * ml_conv[l],
        "ml_gates": _gate_rows(ml_i_bias[l], ml_f_bias[l]),
        "mix_norm": jnp.stack([dn_norm[l].reshape(-1), hg_norm[l].reshape(-1), 0.5 * ml_norm[l].reshape(-1)]
                              + [jnp.zeros((MIX_WIDTH,), _F32)] * 5),
        "w_branch": (0.5 * w_branch[l]).astype(_BF16),
        "w_out": w_out[l].astype(_BF16),
        "w_ff1": w_ff1[l].astype(_BF16),
        "w_ff2": w_ff2[l].astype(_BF16),
    }


def kernel(x_prompt, x_sample, norm_mix_pre, norm_mix_post, norm_ffn_pre, norm_ffn_post, w_in, dn_conv,
           dn_a_log, dn_dt_bias, dn_norm, hg_lb, hg_norm, ml_conv, ml_i_bias, ml_f_bias, ml_norm,
           w_branch, w_out, w_ff1, w_ff2):
    p_layers = jax.nn.softmax(hg_lb.astype(_F32), axis=0)
    lower_bounds = jnp.cumsum(p_layers, axis=0) - p_layers[0]
    y_prompt, y_sample = x_prompt, x_sample
    for l in range(w_in.shape[0]):
        p = _layer_params(l, lower_bounds, norm_mix_pre, norm_mix_post, norm_ffn_pre, norm_ffn_post, w_in,
                          dn_conv, dn_a_log, dn_dt_bias, dn_norm, hg_norm, ml_conv, ml_i_bias, ml_f_bias,
                          ml_norm, w_branch, w_out, w_ff1, w_ff2)
        y_prompt = _layer(y_prompt, p)
        y_sample = _layer(y_sample, p)
    return (y_prompt, y_sample)
```

```python
import jax
import jax.numpy as jnp
from jax import lax
from jax.experimental import pallas as pl
from jax.experimental.pallas import tpu as pltpu

D_MODEL = 1024
N_HEADS = 4
HEAD_DIM = 128
MIX_WIDTH = N_HEADS * HEAD_DIM
N_BRANCH = 3
CONV_K = 5
CONV_HALO = 8
HALO_ROWS = 16
D_FF = 4 * D_MODEL
EPS = 1e-6
LOG2_E = 1.4426950408889634
NEG_BIG = -1e30
CHUNK = 128
MIX_ROWS = 4 * CHUNK
SUB = 4
SOLVE_BASE = 16

GP_OFF = 0
DN_QKV_OFF = 3072
DN_Z_OFF = 4608
ML_QK_OFF = 5120
ML_V_OFF = 6144
ML_O_OFF = 6656
HG_Q_OFF = 7168
HG_I_OFF = 7680
HG_Z_OFF = 8192
NB_COLS = 8960
HG_F_OFF = 0
SM_OFF = 1024
NF_COLS = 1152
SM_BETA, SM_ALPHA, SM_IG, SM_FG = 0, 8, 16, 24
DN_ACT_W = 3 * MIX_WIDTH
ML_ACT_W = 2 * MIX_WIDTH

PROJ_TM = 512
PREP_TM = 256
MERGE_TM = 512
FFN_TM = 512
VMEM_LIMIT = 56 * 1024 * 1024

_F32 = jnp.float32
_BF16 = jnp.bfloat16
_CHAINS = tuple((d, h) for d in range(2) for h in range(N_HEADS))
_DONE = object()


def _bdot(a, b):
    return jnp.dot(a.astype(_BF16), b.astype(_BF16), preferred_element_type=_F32)


def _bdot_nt(a, b):
    return lax.dot_general(a.astype(_BF16), b.astype(_BF16), (((1,), (1,)), ((), ())),
                           preferred_element_type=_F32)


def _bdot_tn(a, b):
    return lax.dot_general(a.astype(_BF16), b.astype(_BF16), (((0,), (0,)), ((), ())),
                           preferred_element_type=_F32)


def _sigmoid(x):
    return 0.5 * jnp.tanh(0.5 * x) + 0.5


def _silu_of_half(h):
    return h * (jnp.tanh(h) + 1.0)


def _two_sigmoid_of_half(h):
    return jnp.tanh(h) + 1.0


def _softplus(x):
    return jnp.maximum(x, 0.0) + jnp.log(1.0 + jnp.exp(-jnp.abs(x)))


def _rmsnorm(x, g):
    return x * lax.rsqrt(jnp.mean(x * x, axis=-1, keepdims=True) + EPS) * g


def _split3(x):
    mask = jnp.uint32(0xFFFF0000)

    def top(v):
        return lax.bitcast_convert_type(lax.bitcast_convert_type(v, jnp.uint32) & mask, _F32)

    hi = top(x)
    r1 = x - hi
    mid = top(r1)
    return hi.astype(_BF16), mid.astype(_BF16), (r1 - mid).astype(_BF16)


def _cumsum_rows(x, rev):
    c, w = x.shape
    row = lax.broadcasted_iota(jnp.int32, (c, c), 0)
    col = lax.broadcasted_iota(jnp.int32, (c, c), 1)
    tri = jnp.where((col >= row) if rev else (col <= row), 1.0, 0.0).astype(_BF16)
    s = jnp.dot(tri, jnp.concatenate(_split3(x), axis=1), preferred_element_type=_F32)
    return s[:, :w] + s[:, w:2 * w] + s[:, 2 * w:]


def _cumsum_lanes(x, rev):
    r, c = x.shape
    row = lax.broadcasted_iota(jnp.int32, (c, c), 0)
    col = lax.broadcasted_iota(jnp.int32, (c, c), 1)
    tri = jnp.where((row >= col) if rev else (row <= col), 1.0, 0.0).astype(_BF16)
    s = jnp.dot(jnp.concatenate(_split3(x), axis=0), tri, preferred_element_type=_F32)
    return s[:r] + s[r:2 * r] + s[2 * r:]


def _head(t, h):
    return t[:, h * HEAD_DIM:(h + 1) * HEAD_DIM]


def _proj_kernel(x_ref, g_ref, wb_ref, wf_ref, ob_ref, of_ref):
    h = _rmsnorm(x_ref[...], g_ref[...]).astype(_BF16)
    ob_ref[...] = jnp.dot(h, wb_ref[...], preferred_element_type=_F32).astype(_BF16)
    of_ref[...] = jnp.dot(h, wf_ref[...], preferred_element_type=_F32)


def _proj(x2, g, w_wide, w_gates):
    t = x2.shape[0]
    tm = min(PROJ_TM, t)
    resident = lambda a: pl.BlockSpec(a.shape, lambda i: (0, 0), pipeline_mode=pl.Buffered(1))
    return pl.pallas_call(
        _proj_kernel,
        grid=(t // tm,),
        in_specs=[pl.BlockSpec((tm, D_MODEL), lambda i: (i, 0)), resident(g), resident(w_wide), resident(w_gates)],
        out_specs=[pl.BlockSpec((tm, NB_COLS), lambda i: (i, 0)), pl.BlockSpec((tm, NF_COLS), lambda i: (i, 0))],
        out_shape=[jax.ShapeDtypeStruct((t, NB_COLS), _BF16), jax.ShapeDtypeStruct((t, NF_COLS), _F32)],
        compiler_params=pltpu.CompilerParams(dimension_semantics=("arbitrary",), vmem_limit_bytes=VMEM_LIMIT),
        name="proj",
    )(x2, g, w_wide, w_gates)


def _conv_act(prev8, cur, next8, w, first, last):
    c = cur.shape[0]
    prev8 = jnp.where(first, 0.0, prev8[HALO_ROWS - CONV_HALO:].astype(_F32))
    next8 = jnp.where(last, 0.0, next8[:CONV_HALO].astype(_F32))
    ext = jnp.concatenate([prev8, cur.astype(_F32), next8], axis=0)
    rows = c + 2 * CONV_HALO
    y = None
    for j in range(CONV_K):
        sh = (CONV_K // 2 - j) % rows
        tap = ext if sh == 0 else pltpu.roll(ext, sh, 0)
        term = w[j:j + 1] * tap[CONV_HALO:CONV_HALO + c]
        y = term if y is None else y + term
    return _silu_of_half(y)


def _prep_kernel(dq, dp, dn, mq, mp, mn, dw_ref, mw_ref, dn_out, ml_out):
    n = pl.program_id(1)
    first = n == 0
    last = n == pl.num_programs(1) - 1
    for h in range(3 * N_HEADS):
        sl = slice(h * HEAD_DIM, (h + 1) * HEAD_DIM)
        t = _conv_act(dp[0, :, sl], dq[0, :, sl], dn[0, :, sl], dw_ref[:, sl], first, last)
        if h < 2 * N_HEADS:
            inv = lax.rsqrt(jnp.sum(t * t, axis=-1, keepdims=True) + EPS)
            t = t * (inv * (HEAD_DIM ** -0.5) if h < N_HEADS else inv)
        dn_out[0, :, sl] = t.astype(_BF16)
    for h in range(2 * N_HEADS):
        sl = slice(h * HEAD_DIM, (h + 1) * HEAD_DIM)
        t = _conv_act(mp[0, :, sl], mq[0, :, sl], mn[0, :, sl], mw_ref[:, sl], first, last)
        ml_out[0, :, sl] = (t * (HEAD_DIM ** -0.5) if h < N_HEADS else t).astype(_BF16)


def _halo_specs(tm, seq, width, off):
    blk = off // width
    r8 = tm // HALO_ROWS
    last8 = seq // HALO_ROWS - 1
    return [pl.BlockSpec((1, tm, width), lambda b, n: (b, n, blk)),
            pl.BlockSpec((1, HALO_ROWS, width), lambda b, n: (b, jnp.maximum(n * r8 - 1, 0), blk)),
            pl.BlockSpec((1, HALO_ROWS, width), lambda b, n: (b, jnp.minimum((n + 1) * r8, last8), blk))]


def _prep_call(zb3, dn_conv, ml_conv):
    b, s, _ = zb3.shape
    tm = min(PREP_TM, s)
    in_specs = (_halo_specs(tm, s, DN_ACT_W, DN_QKV_OFF) + _halo_specs(tm, s, ML_ACT_W, ML_QK_OFF)
                + [_const_spec(dn_conv.shape), _const_spec(ml_conv.shape)])
    return pl.pallas_call(
        _prep_kernel,
        grid=(b, s // tm),
        in_specs=in_specs,
        out_specs=[pl.BlockSpec((1, tm, DN_ACT_W), lambda b, n: (b, n, 0)),
                   pl.BlockSpec((1, tm, ML_ACT_W), lambda b, n: (b, n, 0))],
        out_shape=[jax.ShapeDtypeStruct((b, s, DN_ACT_W), _BF16),
                   jax.ShapeDtypeStruct((b, s, ML_ACT_W), _BF16)],
        compiler_params=pltpu.CompilerParams(dimension_semantics=("arbitrary", "arbitrary"),
                                             vmem_limit_bytes=VMEM_LIMIT),
        name="prep",
    )(zb3, zb3, zb3, zb3, zb3, zb3, dn_conv, ml_conv)


def _pair_specs(c, n_chunks, width, off):
    blk = off // width
    return [pl.BlockSpec((1, c, width), lambda b, n: (b, n, blk)),
            pl.BlockSpec((1, c, width), lambda b, n: (b, n_chunks - 1 - n, blk))]


def _const_spec(shape):
    nd = len(shape)
    return pl.BlockSpec(shape, lambda b, n: (0,) * nd)


def _tri_masks(c, rev):
    row = lax.broadcasted_iota(jnp.int32, (c, c), 0)
    col = lax.broadcasted_iota(jnp.int32, (c, c), 1)
    return ((col >= row), (col > row)) if rev else ((col <= row), (col < row))


def _unit_tri_inverse_many(mats):
    c = mats[0].shape[0]
    row = lax.broadcasted_iota(jnp.int32, (c, c), 0)
    col = lax.broadcasted_iota(jnp.int32, (c, c), 1)
    blk = min(SOLVE_BASE, c)
    eye = jnp.where(row == col, 1.0, 0.0).astype(_F32)
    diag_blk = (row // blk) == (col // blk)
    a0 = [jnp.where(diag_blk, a, 0.0) for a in mats]
    ts = [eye - x for x in a0]
    pws = [_bdot(x, x) for x in a0]
    yield
    k = 2
    while 2 * k < blk:
        boths = [_bdot(jnp.concatenate([t, pw], axis=0), pw) for t, pw in zip(ts, pws)]
        ts = [t + both[:c] for t, both in zip(ts, boths)]
        pws = [both[c:] for both in boths]
        k *= 2
        yield
    ts = [t + _bdot(t, pw) for t, pw in zip(ts, pws)]
    yield
    while blk < c:
        off = ((row // (2 * blk)) == (col // (2 * blk))) & ((row // blk) != (col // blk))
        tl = [_bdot(t, jnp.where(off, a, 0.0)) for t, a in zip(ts, mats)]
        yield
        ts = [t - _bdot(x, t) for t, x in zip(ts, tl)]
        yield
        blk *= 2
    return ts


def _dn_body(rows, af, ab, gates_in, of_ref, ob_ref, s_ref):
    c = CHUNK
    n_g = 2 * N_HEADS
    gates = [dict(cols=cols, gcum=grow[0:n_g], e_tot=grow[n_g:2 * n_g, 0:1], masks=_tri_masks(c, rev))
             for rev, (cols, grow) in zip((False, True), gates_in)]
    acts = (af, ab)
    q, k, kb, vb, eg, a_low, attn = [], [], [], [], [], [], []
    for d, h in _CHAINS:
        g = gates[d]
        r = d * N_HEADS + h
        cols = g["cols"]
        incl, strict = g["masks"]
        qh = acts[d][0, rows[d], h * HEAD_DIM:(h + 1) * HEAD_DIM]
        kh = acts[d][0, rows[d], MIX_WIDTH + h * HEAD_DIM:MIX_WIDTH + (h + 1) * HEAD_DIM]
        vh = acts[d][0, rows[d], 2 * MIX_WIDTH + h * HEAD_DIM:2 * MIX_WIDTH + (h + 1) * HEAD_DIM]
        beta = cols[:, r:r + 1]
        decay = jnp.exp(jnp.where(incl, cols[:, n_g + r:n_g + r + 1] - g["gcum"][r:r + 1, :], NEG_BIG))
        kbh = kh.astype(_F32) * beta
        both = _bdot_nt(jnp.concatenate([kbh.astype(_BF16), qh], axis=0), kh)
        a_low.append(jnp.where(strict, both[:c] * decay, 0.0))
        attn.append(both[c:] * decay)
        q.append(qh)
        k.append(kh)
        kb.append(kbh)
        vb.append(vh.astype(_F32) * beta)
        eg.append(cols[:, 2 * n_g + r:2 * n_g + r + 1])
        if h == N_HEADS - 1:
            yield
    tinv = yield from _unit_tri_inverse_many(a_low)
    sol = [_bdot(t, jnp.concatenate([vb[i], kb[i] * eg[i]], axis=1)) for i, t in enumerate(tinv)]
    yield
    ws_qs = [_bdot(jnp.concatenate([sol[i][:, HEAD_DIM:], q[i].astype(_F32) * eg[i]], axis=0), s_ref[d, h])
             for i, (d, h) in enumerate(_CHAINS)]
    yield
    v_new = [sol[i][:, :HEAD_DIM] - ws_qs[i][:c] for i in range(len(_CHAINS))]
    outs = [ws_qs[i][c:] + _bdot(attn[i], v_new[i]) for i in range(len(_CHAINS))]
    yield
    for i, (d, h) in enumerate(_CHAINS):
        r = d * N_HEADS + h
        er = gates[d]["cols"][:, 3 * n_g + r:3 * n_g + r + 1]
        et = gates[d]["e_tot"][r:r + 1, :]
        s_ref[d, h] = s_ref[d, h] * et + _bdot_tn(k[i].astype(_F32) * er, v_new[i])
    of_ref[0, rows[0], :] = jnp.concatenate(outs[:N_HEADS], axis=1).astype(_BF16)
    ob_ref[0, rows[1], :] = jnp.concatenate(outs[N_HEADS:], axis=1).astype(_BF16)


def _hg_level_masks(c, rev):
    row = lax.broadcasted_iota(jnp.int32, (c, c), 0)
    col = lax.broadcasted_iota(jnp.int32, (c, c), 1)
    levels = []
    blk = c // 2
    while blk >= SUB:
        same_pair = (row // (2 * blk)) == (col // (2 * blk))
        row_hi = (row // blk) % 2 == 1
        col_hi = (col // blk) % 2 == 1
        mask = (same_pair & (~row_hi) & col_hi) if rev else (same_pair & row_hi & (~col_hi))
        levels.append((blk, mask))
        blk //= 2
    return levels


def _hg_band_masks(c, rev):
    row = lax.broadcasted_iota(jnp.int32, (c, c), 0)
    col = lax.broadcasted_iota(jnp.int32, (c, c), 1)
    masks = []
    for dl in range(SUB):
        if rev:
            masks.append((col == row + dl) & ((row % SUB) + dl < SUB))
        else:
            masks.append((col == row - dl) & ((row % SUB) >= dl))
    return masks


def _group_roll(x, shift):
    c, w = x.shape
    return pltpu.roll(x.reshape(c // 8, 8, w), shift % 8, 1).reshape(c, w)


def _pair_reference(b, blk, rev):
    c, w = b.shape
    b3 = b.reshape(c // (2 * blk), 2 * blk, w)
    ref = b3[:, blk:blk + 1, :] if rev else b3[:, blk - 1:blk, :]
    return jnp.broadcast_to(ref, b3.shape).reshape(c, w)


def _hg_body(rows, qf, ff, vf, qb, fb, vb, lb_ref, of_ref, ob_ref, s_ref):
    c = CHUNK
    dirs = []
    for d, (q_ref, f_ref, v_ref) in enumerate(((qf, ff, vf), (qb, fb, vb))):
        rev = d == 1
        lb = lb_ref[d:d + 1, :]
        f_pre = f_ref[0, rows[d], :]
        e = jnp.exp(-jnp.abs(f_pre))
        big = 1.0 / (1.0 + e)
        small = e * big
        pos = f_pre >= 0.0
        q_all = _silu_of_half(q_ref[0, rows[d], :].astype(_F32))
        k_all = (1.0 - lb) * jnp.where(pos, small, big)
        f_all = lb + (1.0 - lb) * jnp.where(pos, big, small)
        logf = jnp.log(f_all)
        b_all = _cumsum_rows(logf, rev)
        btot = jnp.sum(logf, axis=0, keepdims=True)
        dirs.append(dict(rev=rev, q=q_all, k=k_all, f=f_all, v=v_ref[0, rows[d], :],
                         b2=b_all * LOG2_E,
                         qe=q_all * jnp.exp(b_all), ke=k_all * jnp.exp(btot - b_all), e_tot=jnp.exp(btot),
                         levels=_hg_level_masks(c, rev), bands=_hg_band_masks(c, rev)))
        yield
    lvl_ops, band_sums = [], []
    for d, h in _CHAINS:
        g = dirs[d]
        rev = g["rev"]
        q = _head(g["q"], h)
        k = _head(g["k"], h)
        f = _head(g["f"], h)
        b2 = _head(g["b2"], h)
        q16 = q.astype(_BF16)
        k16 = k.astype(_BF16)
        ops = []
        for blk, _ in g["levels"]:
            e = jnp.exp2(-jnp.abs(b2 - _pair_reference(b2, blk, rev))).astype(_BF16)
            ops.append((q16 * e, k16 * e))
        lvl_ops.append(ops)
        sums = []
        qd = q
        for dl in range(SUB):
            if dl > 0:
                qd = qd * (f if dl == 1 else _group_roll(f, (1 - dl) if rev else (dl - 1)))
            k_dl = k if dl == 0 else _group_roll(k, -dl if rev else dl)
            sums.append(jnp.sum(qd * k_dl, axis=-1, keepdims=True))
        band_sums.append(sums)
        if h % 2 == 1:
            yield
    prods = []
    for i in range(len(_CHAINS)):
        prods.append([_bdot_nt(qs, ks) for qs, ks in lvl_ops[i]])
        if i % 2 == 1:
            yield
    attn = []
    for i, (d, h) in enumerate(_CHAINS):
        g = dirs[d]
        a = jnp.zeros((c, c), _F32)
        for (_, mask), prod in zip(g["levels"], prods[i]):
            a = jnp.where(mask, prod, a)
        for dl in range(SUB):
            a = jnp.where(g["bands"][dl], band_sums[i][dl], a)
        attn.append(a)
    yield
    outs = []
    for i, (d, h) in enumerate(_CHAINS):
        g = dirs[d]
        state_t = s_ref[d, h]
        outs.append(_bdot_nt(_head(g["qe"], h), state_t) + _bdot(attn[i], _head(g["v"], h)))
        s_ref[d, h] = state_t * _head(g["e_tot"], h) + _bdot_tn(_head(g["v"], h), _head(g["ke"], h))
        if h % 2 == 1:
            yield
    of_ref[0, rows[0], :] = jnp.concatenate(outs[:N_HEADS], axis=1).astype(_BF16)
    ob_ref[0, rows[1], :] = jnp.concatenate(outs[N_HEADS:], axis=1).astype(_BF16)


def _ml_body(rows, af, vf, ab, vb, gates_in, of_ref, ob_ref, c_ref, m_ref):
    c = CHUNK
    n_g = 2 * N_HEADS
    lane = lax.broadcasted_iota(jnp.int32, (c, HEAD_DIM), 1)
    ones_col = jnp.where(lane == 0, 1.0, 0.0).astype(_BF16)
    gates = [dict(cols=cols, ig=grow[0:n_g], bcum=grow[n_g:2 * n_g], btot=grow[2 * n_g:3 * n_g, 0:1],
                  incl=_tri_masks(c, rev)[0])
             for rev, (cols, grow) in zip((False, True), gates_in)]
    acts = (af, ab)
    vals = (vf, vb)
    q, k, v_ext, qk, qs = [], [], [], [], []
    for d, h in _CHAINS:
        qh = acts[d][0, rows[d], h * HEAD_DIM:(h + 1) * HEAD_DIM]
        kh = acts[d][0, rows[d], MIX_WIDTH + h * HEAD_DIM:MIX_WIDTH + (h + 1) * HEAD_DIM]
        vh = vals[d][0, rows[d], h * HEAD_DIM:(h + 1) * HEAD_DIM]
        q.append(qh)
        k.append(kh)
        v_ext.append(jnp.concatenate([vh.astype(_BF16), ones_col], axis=1))
        qk.append(_bdot_nt(qh, kh))
        qs.append(_bdot(qh, c_ref[d, h]))
        if h == N_HEADS - 1:
            yield
    n_ch = len(_CHAINS)
    logw, inter, logw_end, bt_m = [], [], [], []
    for d, h in _CHAINS:
        g = gates[d]
        col = d * N_HEADS + h
        b_col = g["cols"][:, n_g + col:n_g + col + 1]
        i_col = g["cols"][:, col:col + 1]
        b_row = g["bcum"][col:col + 1, :]
        i_row = g["ig"][col:col + 1, :]
        bt = g["btot"][col:col + 1, :]
        m_old = m_ref[col:col + 1, 0:1]
        logw.append(jnp.where(g["incl"], b_col - b_row + i_row, NEG_BIG))
        inter.append(b_col + m_old)
        logw_end.append(bt - b_col + i_col)
        bt_m.append(bt + m_old)
    yield
    m_row = [jnp.maximum(inter[i], jnp.max(logw[i], axis=-1, keepdims=True)) for i in range(n_ch)]
    m_new = [jnp.maximum(bt_m[i], jnp.max(logw_end[i], axis=0, keepdims=True)) for i in range(n_ch)]
    yield
    p = [qk[i] * jnp.exp(logw[i] - m_row[i]) for i in range(n_ch)]
    yield
    pv = [_bdot(p[i], v_ext[i]) for i in range(n_ch)]
    yield
    outs = []
    for i in range(n_ch):
        nd = jnp.exp(inter[i] - m_row[i]) * qs[i] + pv[i]
        den = nd[:, HEAD_DIM:HEAD_DIM + 1]
        outs.append(nd[:, :HEAD_DIM] / jnp.maximum(jnp.abs(den), jnp.exp(-m_row[i])))
    yield
    kw = [k[i].astype(_F32) * jnp.exp(logw_end[i] - m_new[i]) for i in range(n_ch)]
    for i, (d, h) in enumerate(_CHAINS):
        col = d * N_HEADS + h
        c_ref[d, h] = jnp.exp(bt_m[i] - m_new[i]) * c_ref[d, h] + _bdot_tn(kw[i], v_ext[i])
        m_ref[col:col + 1, :] = jnp.broadcast_to(m_new[i], (1, HEAD_DIM))
    of_ref[0, rows[0], :] = jnp.concatenate(outs[:N_HEADS], axis=1).astype(_BF16)
    ob_ref[0, rows[1], :] = jnp.concatenate(outs[N_HEADS:], axis=1).astype(_BF16)


def _scalar_gates(sf, sb, n_sub, dn_gp, ml_gp, dn_gc, dn_gr, ml_gc, ml_gr):
    c = CHUNK
    n_g = 2 * N_HEADS
    tiles = [(d, jb) for d in range(2) for jb in range(n_sub)]
    smts = [(sf, sb)[d][0, jb * c:(jb + 1) * c, :].T for d, jb in tiles]
    betas = [_sigmoid(t[SM_BETA:SM_BETA + n_g]) for t in smts]
    gs = [-jnp.exp(dn_gp[n_g:2 * n_g]) * _softplus(t[SM_ALPHA:SM_ALPHA + n_g] + dn_gp[0:n_g]) for t in smts]
    igs = [t[SM_IG:SM_IG + n_g] + ml_gp[0:n_g] for t in smts]
    fts = [t[SM_FG:SM_FG + n_g] + ml_gp[n_g:2 * n_g] for t in smts]
    logfs = [jnp.minimum(f, 0.0) - jnp.log(1.0 + jnp.exp(-jnp.abs(f))) for f in fts]
    cums = [_cumsum_lanes(jnp.concatenate(gs[d * n_sub:(d + 1) * n_sub] + logfs[d * n_sub:(d + 1) * n_sub], axis=0),
                          d == 1) for d in range(2)]
    gcums = [cums[d][jb * n_g:(jb + 1) * n_g] for d, jb in tiles]
    bcums = [cums[d][(n_sub + jb) * n_g:(n_sub + jb + 1) * n_g] for d, jb in tiles]
    gtots = [jnp.sum(g, axis=1, keepdims=True) for g in gs]
    btots = [jnp.sum(f, axis=1, keepdims=True) for f in logfs]
    pad = lambda n: jnp.zeros((HEAD_DIM - n * n_g, c), _F32)
    dn_cols = [jnp.concatenate([betas[i], gcums[i], jnp.exp(gcums[i]), jnp.exp(gtots[i] - gcums[i]), pad(4)],
                               axis=0).T for i in range(len(tiles))]
    ml_cols = [jnp.concatenate([igs[i], bcums[i], pad(2)], axis=0).T for i in range(len(tiles))]
    for i, (d, jb) in enumerate(tiles):
        dn_gc[d, jb] = dn_cols[i]
        dn_gr[d, jb] = jnp.concatenate([gcums[i], jnp.broadcast_to(jnp.exp(gtots[i]), (n_g, c))], axis=0)
        ml_gc[d, jb] = ml_cols[i]
        ml_gr[d, jb] = jnp.concatenate([igs[i], bcums[i], jnp.broadcast_to(btots[i], (n_g, c))], axis=0)


def _mix_kernel(sf, sb, dn_af, dn_ab, dn_gp, hg_qf, hg_ff, hg_vf, hg_qb, hg_fb, hg_vb, hg_lb,
                ml_af, ml_vf, ml_ab, ml_vb, ml_gp,
                dn_of, dn_ob, hg_of, hg_ob, ml_of, ml_ob,
                dn_s, hg_s, ml_c, ml_m, dn_gc, dn_gr, ml_gc, ml_gr):
    @pl.when(pl.program_id(1) == 0)
    def _():
        dn_s[...] = jnp.zeros_like(dn_s)
        hg_s[...] = jnp.zeros_like(hg_s)
        ml_c[...] = jnp.zeros_like(ml_c)
        ml_m[...] = jnp.full(ml_m.shape, NEG_BIG, _F32)

    n_sub = sf.shape[1] // CHUNK
    _scalar_gates(sf, sb, n_sub, dn_gp, ml_gp, dn_gc, dn_gr, ml_gc, ml_gr)

    def chunk_step(j, carry):
        jb = (j, n_sub - 1 - j)
        rows = tuple(pl.ds(pl.multiple_of(i * CHUNK, CHUNK), CHUNK) for i in jb)
        dn = _dn_body(rows, dn_af, dn_ab, [(dn_gc[d, jb[d]], dn_gr[d, jb[d]]) for d in range(2)],
                      dn_of, dn_ob, dn_s)
        hg = _hg_body(rows, hg_qf, hg_ff, hg_vf, hg_qb, hg_fb, hg_vb, hg_lb, hg_of, hg_ob, hg_s)
        ml = _ml_body(rows, ml_af, ml_vf, ml_ab, ml_vb, [(ml_gc[d, jb[d]], ml_gr[d, jb[d]]) for d in range(2)],
                      ml_of, ml_ob, ml_c, ml_m)
        pending = [dn, hg, dn, ml]
        while pending:
            for gen in list(pending):
                if gen in pending and next(gen, _DONE) is _DONE:
                    pending = [g for g in pending if g is not gen]
        return carry

    lax.fori_loop(0, n_sub, chunk_step, 0)


def _mix_call(zb3, zf3, dn_act, ml_act, dn_gates, hg_lb, ml_gates):
    b, s, _ = zb3.shape
    c = min(MIX_ROWS, s)
    n_chunks = s // c
    n_sub = c // CHUNK
    pair = lambda width, off: _pair_specs(c, n_chunks, width, off)
    sm, dn_a, ml_a = pair(HEAD_DIM, SM_OFF), pair(DN_ACT_W, 0), pair(ML_ACT_W, 0)
    hg_q, hg_v, ml_v = pair(MIX_WIDTH, HG_Q_OFF), pair(MIX_WIDTH, HG_I_OFF), pair(MIX_WIDTH, ML_V_OFF)
    hg_ff = pair(MIX_WIDTH, HG_F_OFF)[0]
    hg_fb = pair(MIX_WIDTH, HG_F_OFF + MIX_WIDTH)[1]
    in_specs = [sm[0], sm[1], dn_a[0], dn_a[1], _const_spec(dn_gates.shape),
                hg_q[0], hg_ff, hg_v[0], hg_q[1], hg_fb, hg_v[1], _const_spec(hg_lb.shape),
                ml_a[0], ml_v[0], ml_a[1], ml_v[1], _const_spec(ml_gates.shape)]
    args = (zf3, zf3, dn_act, dn_act, dn_gates, zb3, zf3, zb3, zb3, zf3, zb3, hg_lb,
            ml_act, zb3, ml_act, zb3, ml_gates)
    out = jax.ShapeDtypeStruct((b, s, MIX_WIDTH), _BF16)
    state = (2, N_HEADS, HEAD_DIM, HEAD_DIM)
    return pl.pallas_call(
        _mix_kernel,
        grid=(b, n_chunks),
        in_specs=in_specs,
        out_specs=pair(MIX_WIDTH, 0) * 3,
        out_shape=[out] * 6,
        scratch_shapes=[pltpu.VMEM(state, _F32), pltpu.VMEM(state, _F32),
                        pltpu.VMEM((2, N_HEADS, HEAD_DIM, 2 * HEAD_DIM), _F32),
                        pltpu.VMEM((2 * N_HEADS, HEAD_DIM), _F32),
                        pltpu.VMEM((2, n_sub, CHUNK, HEAD_DIM), _F32), pltpu.VMEM((2, n_sub, 4 * N_HEADS, CHUNK), _F32),
                        pltpu.VMEM((2, n_sub, CHUNK, HEAD_DIM), _F32), pltpu.VMEM((2, n_sub, 6 * N_HEADS, CHUNK), _F32)],
        compiler_params=pltpu.CompilerParams(dimension_semantics=("arbitrary", "arbitrary"),
                                             vmem_limit_bytes=VMEM_LIMIT),
        name="mixers",
    )(*args)


def _head_norm(o, w):
    parts = []
    for h in range(N_HEADS):
        seg = _head(o, h)
        parts.append(seg * lax.rsqrt(jnp.mean(seg * seg, axis=-1, keepdims=True) + EPS))
    return jnp.concatenate(parts, axis=1) * w


def _merge_kernel(x_ref, dnf, dnb, hgf, hgb, mlf, mlb, dnz, hgz, mlo, gp_ref,
                  nw_ref, wbr_ref, wout_ref, gpost_ref, o_ref):
    both = lambda f, b: f[...].astype(_F32) + b[...].astype(_F32)
    branches = (
        _head_norm(both(dnf, dnb), nw_ref[0:1, :]) * _silu_of_half(dnz[...].astype(_F32)),
        _head_norm(both(hgf, hgb), nw_ref[1:2, :]) * _silu_of_half(hgz[...].astype(_F32)),
        _head_norm(both(mlf, mlb), nw_ref[2:3, :]) * _two_sigmoid_of_half(mlo[...].astype(_F32)),
    )
    merged = None
    for i, br in enumerate(branches):
        gate = _two_sigmoid_of_half(gp_ref[:, i * D_MODEL:(i + 1) * D_MODEL].astype(_F32))
        term = gate * _bdot(br, wbr_ref[i])
        merged = term if merged is None else merged + term
    y = _bdot(merged, wout_ref[...])
    o_ref[...] = x_ref[...] + _rmsnorm(y, gpost_ref[...])


def _merge_call(x2, zb2, mix_outs, norm_w, w_branch, w_out, g_post):
    t = x2.shape[0]
    tm = min(MERGE_TM, t)
    row = lambda w: pl.BlockSpec((tm, w), lambda i: (i, 0))
    zblk = lambda w, off: pl.BlockSpec((tm, w), lambda i: (i, off // w))
    full = lambda a: pl.BlockSpec(a.shape, lambda i: (0,) * a.ndim)
    in_specs = ([row(D_MODEL)] + [row(MIX_WIDTH)] * 6
                + [zblk(MIX_WIDTH, DN_Z_OFF), zblk(MIX_WIDTH, HG_Z_OFF), zblk(MIX_WIDTH, ML_O_OFF),
                   zblk(N_BRANCH * D_MODEL, GP_OFF)]
                + [full(norm_w), full(w_branch), full(w_out), full(g_post)])
    return pl.pallas_call(
        _merge_kernel,
        grid=(t // tm,),
        in_specs=in_specs,
        out_specs=row(D_MODEL),
        out_shape=jax.ShapeDtypeStruct((t, D_MODEL), _F32),
        compiler_params=pltpu.CompilerParams(dimension_semantics=("arbitrary",),
                                             vmem_limit_bytes=VMEM_LIMIT),
        name="merge",
    )(x2, *mix_outs, zb2, zb2, zb2, zb2, norm_w, w_branch, w_out, g_post)


def _ffn_kernel(x_ref, gpre_ref, w1_ref, w2_ref, gpost_ref, o_ref):
    x = x_ref[...]
    u = jnp.maximum(_bdot(_rmsnorm(x, gpre_ref[...]), w1_ref[...]), 0.0)
    y = _bdot(u * u, w2_ref[...])
    o_ref[...] = x + _rmsnorm(y, gpost_ref[...])


def _ffn_call(x2, g_pre, w1, w2, g_post):
    t = x2.shape[0]
    tm = min(FFN_TM, t)
    row = pl.BlockSpec((tm, D_MODEL), lambda i: (i, 0))
    full = lambda a: pl.BlockSpec(a.shape, lambda i: (0,) * a.ndim, pipeline_mode=pl.Buffered(1))
    return pl.pallas_call(
        _ffn_kernel,
        grid=(t // tm,),
        in_specs=[row, full(g_pre), full(w1), full(w2), full(g_post)],
        out_specs=row,
        out_shape=jax.ShapeDtypeStruct((t, D_MODEL), _F32),
        compiler_params=pltpu.CompilerParams(dimension_semantics=("arbitrary",),
                                             vmem_limit_bytes=VMEM_LIMIT),
        name="ffn",
    )(x2, g_pre, w1, w2, g_post)


def _permute_w_in(w_in):
    m = MIX_WIDTH
    o_dn, o_sm_dn, o_hg, o_ml, o_sm_ml, o_gp = 0, 4 * m, 4 * m + 16, 9 * m + 16, 13 * m + 16, 13 * m + 32
    wide = [
        0.5 * w_in[:, o_gp:o_gp + N_BRANCH * D_MODEL],
        w_in[:, o_dn:o_dn + 3 * m],
        0.5 * w_in[:, o_dn + 3 * m:o_dn + 4 * m],
        w_in[:, o_ml:o_ml + 3 * m],
        0.5 * w_in[:, o_ml + 3 * m:o_ml + 4 * m],
        0.5 * w_in[:, o_hg:o_hg + m],
        w_in[:, o_hg + 3 * m:o_hg + 4 * m],
        0.5 * w_in[:, o_hg + 4 * m:o_hg + 5 * m],
        jnp.zeros((w_in.shape[0], NB_COLS - HG_Z_OFF - m), w_in.dtype),
    ]
    gates = [
        w_in[:, o_hg + m:o_hg + 3 * m],
        w_in[:, o_sm_dn:o_sm_dn + 16],
        w_in[:, o_sm_ml:o_sm_ml + 16],
        jnp.zeros((w_in.shape[0], NF_COLS - SM_OFF - 32), w_in.dtype),
    ]
    return jnp.concatenate(wide, axis=1).astype(_BF16), jnp.concatenate(gates, axis=1).astype(_BF16)


def _gate_rows(first, second):
    vals = jnp.concatenate([first.astype(_F32).reshape(-1), second.astype(_F32).reshape(-1)])
    return jnp.broadcast_to(vals[:, None], (vals.shape[0], HEAD_DIM))


def _layer(x3, p):
    b, s, _ = x3.shape
    x2 = x3.reshape(b * s, D_MODEL)
    w_wide, w_gates = p["w_in"]
    zb2, zf2 = _proj(x2, p["norm_mix_pre"], w_wide, w_gates)
    zb3 = zb2.reshape(b, s, NB_COLS)
    zf3 = zf2.reshape(b, s, NF_COLS)
    dn_act, ml_act = _prep_call(zb3, p["dn_conv"], p["ml_conv"])
    mix = _mix_call(zb3, zf3, dn_act, ml_act, p["dn_gates"], p["hg_lb"], p["ml_gates"])
    mix = [a.reshape(b * s, MIX_WIDTH) for a in mix]
    x2 = _merge_call(x2, zb2, mix, p["mix_norm"], p["w_branch"], p["w_out"], p["norm_mix_post"])
    x2 = _ffn_call(x2, p["norm_ffn_pre"], p["w_ff1"], p["w_ff2"], p["norm_ffn_post"])
    return x2.reshape(b, s, D_MODEL)


def _layer_params(l, lower_bounds, norm_mix_pre, norm_mix_post, norm_ffn_pre, norm_ffn_post, w_in, dn_conv,
                  dn_a_log, dn_dt_bias, dn_norm, hg_norm, ml_conv, ml_i_bias, ml_f_bias, ml_norm,
                  w_branch, w_out, w_ff1, w_ff2):
    return {
        "norm_mix_pre": norm_mix_pre[l].reshape(1, D_MODEL),
        "norm_mix_post": norm_mix_post[l].reshape(1, D_MODEL),
        "norm_ffn_pre": norm_ffn_pre[l].reshape(1, D_MODEL),
        "norm_ffn_post": norm_ffn_post[l].reshape(1, D_MODEL),
        "w_in": _permute_w_in(w_in[l]),
        "dn_conv": 0.5 * dn_conv[l],
        "dn_gates": _gate_rows(dn_dt_bias[l], dn_a_log[l]),
        "hg_lb": lower_bounds[l],
        "ml_conv": 0.5 * ml_conv[l],
        "ml_gates": _gate_rows(ml_i_bias[l], ml_f_bias[l]),
        "mix_norm": jnp.stack([dn_norm[l].reshape(-1), hg_norm[l].reshape(-1), 0.5 * ml_norm[l].reshape(-1)]
                              + [jnp.zeros((MIX_WIDTH,), _F32)] * 5),
        "w_branch": (0.5 * w_branch[l]).astype(_BF16),
        "w_out": w_out[l].astype(_BF16),
        "w_ff1": w_ff1[l].astype(_BF16),
        "w_ff2": w_ff2[l].astype(_BF16),
    }


def kernel(x_prompt, x_sample, norm_mix_pre, norm_mix_post, norm_ffn_pre, norm_ffn_post, w_in, dn_conv,
           dn_a_log, dn_dt_bias, dn_norm, hg_lb, hg_norm, ml_conv, ml_i_bias, ml_f_bias, ml_norm,
           w_branch, w_out, w_ff1, w_ff2):
    p_layers = jax.nn.softmax(hg_lb.astype(_F32), axis=0)
    lower_bounds = jnp.cumsum(p_layers, axis=0) - p_layers[0]
    y_prompt, y_sample = x_prompt, x_sample
    for l in range(w_in.shape[0]):
        p = _layer_params(l, lower_bounds, norm_mix_pre, norm_mix_post, norm_ffn_pre, norm_ffn_post, w_in,
                          dn_conv, dn_a_log, dn_dt_bias, dn_norm, hg_norm, ml_conv, ml_i_bias, ml_f_bias,
                          ml_norm, w_branch, w_out, w_ff1, w_ff2)
        y_prompt = _layer(y_prompt, p)
        y_sample = _layer(y_sample, p)
    return (y_prompt, y_sample)
```

```python
import jax
import jax.numpy as jnp
from jax import lax
from jax.experimental import pallas as pl
from jax.experimental.pallas import tpu as pltpu

D_MODEL = 1024
N_HEADS = 4
HEAD_DIM = 128
MIX_WIDTH = N_HEADS * HEAD_DIM
N_BRANCH = 3
CONV_K = 5
CONV_HALO = 8
HALO_ROWS = 16
D_FF = 4 * D_MODEL
EPS = 1e-6
LOG2_E = 1.4426950408889634
NEG_BIG = -1e30
CHUNK = 128
MIX_ROWS = 4 * CHUNK
SUB = 4
SOLVE_BASE = 16

GP_OFF = 0
DN_QKV_OFF = 3072
DN_Z_OFF = 4608
ML_QK_OFF = 5120
ML_V_OFF = 6144
ML_O_OFF = 6656
HG_Q_OFF = 7168
HG_I_OFF = 7680
HG_Z_OFF = 8192
NB_COLS = 8960
HG_F_OFF = 0
SM_OFF = 1024
NF_COLS = 1152
SM_BETA, SM_ALPHA, SM_IG, SM_FG = 0, 8, 16, 24
DN_ACT_W = 3 * MIX_WIDTH
ML_ACT_W = 2 * MIX_WIDTH

PROJ_TM = 512
PREP_TM = 256
MERGE_TM = 512
FFN_TM = 512
FFN_SLAB = 1024
VMEM_LIMIT = 56 * 1024 * 1024

_F32 = jnp.float32
_BF16 = jnp.bfloat16
_CHAINS = tuple((d, h) for d in range(2) for h in range(N_HEADS))
_DONE = object()


def _bdot(a, b):
    return jnp.dot(a.astype(_BF16), b.astype(_BF16), preferred_element_type=_F32)


def _bdot_nt(a, b):
    return lax.dot_general(a.astype(_BF16), b.astype(_BF16), (((1,), (1,)), ((), ())),
                           preferred_element_type=_F32)


def _bdot_tn(a, b):
    return lax.dot_general(a.astype(_BF16), b.astype(_BF16), (((0,), (0,)), ((), ())),
                           preferred_element_type=_F32)


def _sigmoid(x):
    return 0.5 * jnp.tanh(0.5 * x) + 0.5


def _silu_of_half(h):
    return h * (jnp.tanh(h) + 1.0)


def _two_sigmoid_of_half(h):
    return jnp.tanh(h) + 1.0


def _softplus(x):
    return jnp.maximum(x, 0.0) + jnp.log(1.0 + jnp.exp(-jnp.abs(x)))


def _rmsnorm(x, g):
    return x * lax.rsqrt(jnp.mean(x * x, axis=-1, keepdims=True) + EPS) * g


def _split3(x):
    mask = jnp.uint32(0xFFFF0000)

    def top(v):
        return lax.bitcast_convert_type(lax.bitcast_convert_type(v, jnp.uint32) & mask, _F32)

    hi = top(x)
    r1 = x - hi
    mid = top(r1)
    return hi.astype(_BF16), mid.astype(_BF16), (r1 - mid).astype(_BF16)


def _cumsum_rows(x, rev):
    c, w = x.shape
    row = lax.broadcasted_iota(jnp.int32, (c, c), 0)
    col = lax.broadcasted_iota(jnp.int32, (c, c), 1)
    tri = jnp.where((col >= row) if rev else (col <= row), 1.0, 0.0).astype(_BF16)
    s = jnp.dot(tri, jnp.concatenate(_split3(x), axis=1), preferred_element_type=_F32)
    return s[:, :w] + s[:, w:2 * w] + s[:, 2 * w:]


def _cumsum_lanes(x, rev):
    r, c = x.shape
    row = lax.broadcasted_iota(jnp.int32, (c, c), 0)
    col = lax.broadcasted_iota(jnp.int32, (c, c), 1)
    tri = jnp.where((row >= col) if rev else (row <= col), 1.0, 0.0).astype(_BF16)
    s = jnp.dot(jnp.concatenate(_split3(x), axis=0), tri, preferred_element_type=_F32)
    return s[:r] + s[r:2 * r] + s[2 * r:]


def _head(t, h):
    return t[:, h * HEAD_DIM:(h + 1) * HEAD_DIM]


def _proj_kernel(x_ref, g_ref, wb_ref, wf_ref, ob_ref, of_ref):
    h = _rmsnorm(x_ref[...], g_ref[...]).astype(_BF16)
    ob_ref[...] = jnp.dot(h, wb_ref[...], preferred_element_type=_F32).astype(_BF16)
    of_ref[...] = jnp.dot(h, wf_ref[...], preferred_element_type=_F32)


def _proj(x2, g, w_wide, w_gates):
    t = x2.shape[0]
    tm = min(PROJ_TM, t)
    resident = lambda a: pl.BlockSpec(a.shape, lambda i: (0, 0), pipeline_mode=pl.Buffered(1))
    return pl.pallas_call(
        _proj_kernel,
        grid=(t // tm,),
        in_specs=[pl.BlockSpec((tm, D_MODEL), lambda i: (i, 0)), resident(g), resident(w_wide), resident(w_gates)],
        out_specs=[pl.BlockSpec((tm, NB_COLS), lambda i: (i, 0)), pl.BlockSpec((tm, NF_COLS), lambda i: (i, 0))],
        out_shape=[jax.ShapeDtypeStruct((t, NB_COLS), _BF16), jax.ShapeDtypeStruct((t, NF_COLS), _F32)],
        compiler_params=pltpu.CompilerParams(dimension_semantics=("arbitrary",), vmem_limit_bytes=VMEM_LIMIT),
        name="proj",
    )(x2, g, w_wide, w_gates)


def _conv_act(prev8, cur, next8, w, first, last):
    c = cur.shape[0]
    prev8 = jnp.where(first, 0.0, prev8[HALO_ROWS - CONV_HALO:].astype(_F32))
    next8 = jnp.where(last, 0.0, next8[:CONV_HALO].astype(_F32))
    ext = jnp.concatenate([prev8, cur.astype(_F32), next8], axis=0)
    rows = c + 2 * CONV_HALO
    y = None
    for j in range(CONV_K):
        sh = (CONV_K // 2 - j) % rows
        tap = ext if sh == 0 else pltpu.roll(ext, sh, 0)
        term = w[j:j + 1] * tap[CONV_HALO:CONV_HALO + c]
        y = term if y is None else y + term
    return _silu_of_half(y)


def _prep_kernel(dq, dp, dn, mq, mp, mn, dw_ref, mw_ref, dn_out, ml_out):
    n = pl.program_id(1)
    first = n == 0
    last = n == pl.num_programs(1) - 1
    for h in range(3 * N_HEADS):
        sl = slice(h * HEAD_DIM, (h + 1) * HEAD_DIM)
        t = _conv_act(dp[0, :, sl], dq[0, :, sl], dn[0, :, sl], dw_ref[:, sl], first, last)
        if h < 2 * N_HEADS:
            inv = lax.rsqrt(jnp.sum(t * t, axis=-1, keepdims=True) + EPS)
            t = t * (inv * (HEAD_DIM ** -0.5) if h < N_HEADS else inv)
        dn_out[0, :, sl] = t.astype(_BF16)
    for h in range(2 * N_HEADS):
        sl = slice(h * HEAD_DIM, (h + 1) * HEAD_DIM)
        t = _conv_act(mp[0, :, sl], mq[0, :, sl], mn[0, :, sl], mw_ref[:, sl], first, last)
        ml_out[0, :, sl] = (t * (HEAD_DIM ** -0.5) if h < N_HEADS else t).astype(_BF16)


def _halo_specs(tm, seq, width, off):
    blk = off // width
    r8 = tm // HALO_ROWS
    last8 = seq // HALO_ROWS - 1
    return [pl.BlockSpec((1, tm, width), lambda b, n: (b, n, blk)),
            pl.BlockSpec((1, HALO_ROWS, width), lambda b, n: (b, jnp.maximum(n * r8 - 1, 0), blk)),
            pl.BlockSpec((1, HALO_ROWS, width), lambda b, n: (b, jnp.minimum((n + 1) * r8, last8), blk))]


def _prep_call(zb3, dn_conv, ml_conv):
    b, s, _ = zb3.shape
    tm = min(PREP_TM, s)
    in_specs = (_halo_specs(tm, s, DN_ACT_W, DN_QKV_OFF) + _halo_specs(tm, s, ML_ACT_W, ML_QK_OFF)
                + [_const_spec(dn_conv.shape), _const_spec(ml_conv.shape)])
    return pl.pallas_call(
        _prep_kernel,
        grid=(b, s // tm),
        in_specs=in_specs,
        out_specs=[pl.BlockSpec((1, tm, DN_ACT_W), lambda b, n: (b, n, 0)),
                   pl.BlockSpec((1, tm, ML_ACT_W), lambda b, n: (b, n, 0))],
        out_shape=[jax.ShapeDtypeStruct((b, s, DN_ACT_W), _BF16),
                   jax.ShapeDtypeStruct((b, s, ML_ACT_W), _BF16)],
        compiler_params=pltpu.CompilerParams(dimension_semantics=("arbitrary", "arbitrary"),
                                             vmem_limit_bytes=VMEM_LIMIT),
        name="prep",
    )(zb3, zb3, zb3, zb3, zb3, zb3, dn_conv, ml_conv)


def _pair_specs(c, n_chunks, width, off):
    blk = off // width
    return [pl.BlockSpec((1, c, width), lambda b, n: (b, n, blk)),
            pl.BlockSpec((1, c, width), lambda b, n: (b, n_chunks - 1 - n, blk))]


def _const_spec(shape):
    nd = len(shape)
    return pl.BlockSpec(shape, lambda b, n: (0,) * nd)


def _tri_masks(c, rev):
    row = lax.broadcasted_iota(jnp.int32, (c, c), 0)
    col = lax.broadcasted_iota(jnp.int32, (c, c), 1)
    return ((col >= row), (col > row)) if rev else ((col <= row), (col < row))


def _unit_tri_inverse_many(mats):
    c = mats[0].shape[0]
    row = lax.broadcasted_iota(jnp.int32, (c, c), 0)
    col = lax.broadcasted_iota(jnp.int32, (c, c), 1)
    blk = min(SOLVE_BASE, c)
    eye = jnp.where(row == col, 1.0, 0.0).astype(_F32)
    diag_blk = (row // blk) == (col // blk)
    a0 = [jnp.where(diag_blk, a, 0.0) for a in mats]
    ts = [eye - x for x in a0]
    pws = [_bdot(x, x) for x in a0]
    yield
    k = 2
    while 2 * k < blk:
        boths = [_bdot(jnp.concatenate([t, pw], axis=0), pw) for t, pw in zip(ts, pws)]
        ts = [t + both[:c] for t, both in zip(ts, boths)]
        pws = [both[c:] for both in boths]
        k *= 2
        yield
    ts = [t + _bdot(t, pw) for t, pw in zip(ts, pws)]
    yield
    while blk < c:
        off = ((row // (2 * blk)) == (col // (2 * blk))) & ((row // blk) != (col // blk))
        tl = [_bdot(t, jnp.where(off, a, 0.0)) for t, a in zip(ts, mats)]
        yield
        ts = [t - _bdot(x, t) for t, x in zip(ts, tl)]
        yield
        blk *= 2
    return ts


def _dn_body(rows, af, ab, gates_in, of_ref, ob_ref, s_ref):
    c = CHUNK
    n_g = 2 * N_HEADS
    gates = [dict(cols=cols, gcum=grow[0:n_g], e_tot=grow[n_g:2 * n_g, 0:1], masks=_tri_masks(c, rev))
             for rev, (cols, grow) in zip((False, True), gates_in)]
    acts = (af, ab)
    q, k, kb, vb, eg, a_low, attn = [], [], [], [], [], [], []
    for d, h in _CHAINS:
        g = gates[d]
        r = d * N_HEADS + h
        cols = g["cols"]
        incl, strict = g["masks"]
        qh = acts[d][0, rows[d], h * HEAD_DIM:(h + 1) * HEAD_DIM]
        kh = acts[d][0, rows[d], MIX_WIDTH + h * HEAD_DIM:MIX_WIDTH + (h + 1) * HEAD_DIM]
        vh = acts[d][0, rows[d], 2 * MIX_WIDTH + h * HEAD_DIM:2 * MIX_WIDTH + (h + 1) * HEAD_DIM]
        beta = cols[:, r:r + 1]
        decay = jnp.exp(jnp.where(incl, cols[:, n_g + r:n_g + r + 1] - g["gcum"][r:r + 1, :], NEG_BIG))
        kbh = kh.astype(_F32) * beta
        both = _bdot_nt(jnp.concatenate([kbh.astype(_BF16), qh], axis=0), kh)
        a_low.append(jnp.where(strict, both[:c] * decay, 0.0))
        attn.append(both[c:] * decay)
        q.append(qh)
        k.append(kh)
        kb.append(kbh)
        vb.append(vh.astype(_F32) * beta)
        eg.append(cols[:, 2 * n_g + r:2 * n_g + r + 1])
        if h == N_HEADS - 1:
            yield
    tinv = yield from _unit_tri_inverse_many(a_low)
    sol = [_bdot(t, jnp.concatenate([vb[i], kb[i] * eg[i]], axis=1)) for i, t in enumerate(tinv)]
    yield
    ws_qs = [_bdot(jnp.concatenate([sol[i][:, HEAD_DIM:], q[i].astype(_F32) * eg[i]], axis=0), s_ref[d, h])
             for i, (d, h) in enumerate(_CHAINS)]
    yield
    v_new = [sol[i][:, :HEAD_DIM] - ws_qs[i][:c] for i in range(len(_CHAINS))]
    outs = [ws_qs[i][c:] + _bdot(attn[i], v_new[i]) for i in range(len(_CHAINS))]
    yield
    for i, (d, h) in enumerate(_CHAINS):
        r = d * N_HEADS + h
        er = gates[d]["cols"][:, 3 * n_g + r:3 * n_g + r + 1]
        et = gates[d]["e_tot"][r:r + 1, :]
        s_ref[d, h] = s_ref[d, h] * et + _bdot_tn(k[i].astype(_F32) * er, v_new[i])
    of_ref[0, rows[0], :] = jnp.concatenate(outs[:N_HEADS], axis=1).astype(_BF16)
    ob_ref[0, rows[1], :] = jnp.concatenate(outs[N_HEADS:], axis=1).astype(_BF16)


def _hg_level_masks(c, rev):
    row = lax.broadcasted_iota(jnp.int32, (c, c), 0)
    col = lax.broadcasted_iota(jnp.int32, (c, c), 1)
    levels = []
    blk = c // 2
    while blk >= SUB:
        same_pair = (row // (2 * blk)) == (col // (2 * blk))
        row_hi = (row // blk) % 2 == 1
        col_hi = (col // blk) % 2 == 1
        mask = (same_pair & (~row_hi) & col_hi) if rev else (same_pair & row_hi & (~col_hi))
        levels.append((blk, mask))
        blk //= 2
    return levels


def _hg_band_masks(c, rev):
    row = lax.broadcasted_iota(jnp.int32, (c, c), 0)
    col = lax.broadcasted_iota(jnp.int32, (c, c), 1)
    masks = []
    for dl in range(SUB):
        if rev:
            masks.append((col == row + dl) & ((row % SUB) + dl < SUB))
        else:
            masks.append((col == row - dl) & ((row % SUB) >= dl))
    return masks


def _group_roll(x, shift):
    c, w = x.shape
    return pltpu.roll(x.reshape(c // 8, 8, w), shift % 8, 1).reshape(c, w)


def _pair_reference(b, blk, rev):
    c, w = b.shape
    b3 = b.reshape(c // (2 * blk), 2 * blk, w)
    ref = b3[:, blk:blk + 1, :] if rev else b3[:, blk - 1:blk, :]
    return jnp.broadcast_to(ref, b3.shape).reshape(c, w)


def _hg_body(rows, qf, ff, vf, qb, fb, vb, lb_ref, of_ref, ob_ref, s_ref):
    c = CHUNK
    dirs = []
    for d, (q_ref, f_ref, v_ref) in enumerate(((qf, ff, vf), (qb, fb, vb))):
        rev = d == 1
        lb = lb_ref[d:d + 1, :]
        f_pre = f_ref[0, rows[d], :]
        e = jnp.exp(-jnp.abs(f_pre))
        big = 1.0 / (1.0 + e)
        small = e * big
        pos = f_pre >= 0.0
        q_all = _silu_of_half(q_ref[0, rows[d], :].astype(_F32))
        k_all = (1.0 - lb) * jnp.where(pos, small, big)
        f_all = lb + (1.0 - lb) * jnp.where(pos, big, small)
        logf = jnp.log(f_all)
        b_all = _cumsum_rows(logf, rev)
        btot = jnp.sum(logf, axis=0, keepdims=True)
        dirs.append(dict(rev=rev, q=q_all, k=k_all, f=f_all, v=v_ref[0, rows[d], :],
                         b2=b_all * LOG2_E,
                         qe=q_all * jnp.exp(b_all), ke=k_all * jnp.exp(btot - b_all), e_tot=jnp.exp(btot),
                         levels=_hg_level_masks(c, rev), bands=_hg_band_masks(c, rev)))
        yield
    lvl_ops, band_sums = [], []
    for d, h in _CHAINS:
        g = dirs[d]
        rev = g["rev"]
        q = _head(g["q"], h)
        k = _head(g["k"], h)
        f = _head(g["f"], h)
        b2 = _head(g["b2"], h)
        q16 = q.astype(_BF16)
        k16 = k.astype(_BF16)
        ops = []
        for blk, _ in g["levels"]:
            e = jnp.exp2(-jnp.abs(b2 - _pair_reference(b2, blk, rev))).astype(_BF16)
            ops.append((q16 * e, k16 * e))
        lvl_ops.append(ops)
        sums = []
        qd = q
        for dl in range(SUB):
            if dl > 0:
                qd = qd * (f if dl == 1 else _group_roll(f, (1 - dl) if rev else (dl - 1)))
            k_dl = k if dl == 0 else _group_roll(k, -dl if rev else dl)
            sums.append(jnp.sum(qd * k_dl, axis=-1, keepdims=True))
        band_sums.append(sums)
        if h % 2 == 1:
            yield
    prods = []
    for i in range(len(_CHAINS)):
        prods.append([_bdot_nt(qs, ks) for qs, ks in lvl_ops[i]])
        if i % 2 == 1:
            yield
    attn = []
    for i, (d, h) in enumerate(_CHAINS):
        g = dirs[d]
        a = jnp.zeros((c, c), _F32)
        for (_, mask), prod in zip(g["levels"], prods[i]):
            a = jnp.where(mask, prod, a)
        for dl in range(SUB):
            a = jnp.where(g["bands"][dl], band_sums[i][dl], a)
        attn.append(a)
    yield
    outs = []
    for i, (d, h) in enumerate(_CHAINS):
        g = dirs[d]
        state_t = s_ref[d, h]
        outs.append(_bdot_nt(_head(g["qe"], h), state_t) + _bdot(attn[i], _head(g["v"], h)))
        s_ref[d, h] = state_t * _head(g["e_tot"], h) + _bdot_tn(_head(g["v"], h), _head(g["ke"], h))
        if h % 2 == 1:
            yield
    of_ref[0, rows[0], :] = jnp.concatenate(outs[:N_HEADS], axis=1).astype(_BF16)
    ob_ref[0, rows[1], :] = jnp.concatenate(outs[N_HEADS:], axis=1).astype(_BF16)


def _ml_body(rows, af, vf, ab, vb, gates_in, of_ref, ob_ref, c_ref, m_ref):
    c = CHUNK
    n_g = 2 * N_HEADS
    lane = lax.broadcasted_iota(jnp.int32, (c, HEAD_DIM), 1)
    ones_col = jnp.where(lane == 0, 1.0, 0.0).astype(_BF16)
    gates = [dict(cols=cols, ig=grow[0:n_g], bcum=grow[n_g:2 * n_g], btot=grow[2 * n_g:3 * n_g, 0:1],
                  incl=_tri_masks(c, rev)[0])
             for rev, (cols, grow) in zip((False, True), gates_in)]
    acts = (af, ab)
    vals = (vf, vb)
    q, k, v_ext, qk, qs = [], [], [], [], []
    for d, h in _CHAINS:
        qh = acts[d][0, rows[d], h * HEAD_DIM:(h + 1) * HEAD_DIM]
        kh = acts[d][0, rows[d], MIX_WIDTH + h * HEAD_DIM:MIX_WIDTH + (h + 1) * HEAD_DIM]
        vh = vals[d][0, rows[d], h * HEAD_DIM:(h + 1) * HEAD_DIM]
        q.append(qh)
        k.append(kh)
        v_ext.append(jnp.concatenate([vh.astype(_BF16), ones_col], axis=1))
        qk.append(_bdot_nt(qh, kh))
        qs.append(_bdot(qh, c_ref[d, h]))
        if h == N_HEADS - 1:
            yield
    n_ch = len(_CHAINS)
    logw, inter, logw_end, bt_m = [], [], [], []
    for d, h in _CHAINS:
        g = gates[d]
        col = d * N_HEADS + h
        b_col = g["cols"][:, n_g + col:n_g + col + 1]
        i_col = g["cols"][:, col:col + 1]
        b_row = g["bcum"][col:col + 1, :]
        i_row = g["ig"][col:col + 1, :]
        bt = g["btot"][col:col + 1, :]
        m_old = m_ref[col:col + 1, 0:1]
        logw.append(jnp.where(g["incl"], b_col - b_row + i_row, NEG_BIG))
        inter.append(b_col + m_old)
        logw_end.append(bt - b_col + i_col)
        bt_m.append(bt + m_old)
    yield
    m_row = [jnp.maximum(inter[i], jnp.max(logw[i], axis=-1, keepdims=True)) for i in range(n_ch)]
    m_new = [jnp.maximum(bt_m[i], jnp.max(logw_end[i], axis=0, keepdims=True)) for i in range(n_ch)]
    yield
    p = [qk[i] * jnp.exp(logw[i] - m_row[i]) for i in range(n_ch)]
    yield
    pv = [_bdot(p[i], v_ext[i]) for i in range(n_ch)]
    yield
    outs = []
    for i in range(n_ch):
        nd = jnp.exp(inter[i] - m_row[i]) * qs[i] + pv[i]
        den = nd[:, HEAD_DIM:HEAD_DIM + 1]
        outs.append(nd[:, :HEAD_DIM] / jnp.maximum(jnp.abs(den), jnp.exp(-m_row[i])))
    yield
    kw = [k[i].astype(_F32) * jnp.exp(logw_end[i] - m_new[i]) for i in range(n_ch)]
    for i, (d, h) in enumerate(_CHAINS):
        col = d * N_HEADS + h
        c_ref[d, h] = jnp.exp(bt_m[i] - m_new[i]) * c_ref[d, h] + _bdot_tn(kw[i], v_ext[i])
        m_ref[col:col + 1, :] = jnp.broadcast_to(m_new[i], (1, HEAD_DIM))
    of_ref[0, rows[0], :] = jnp.concatenate(outs[:N_HEADS], axis=1).astype(_BF16)
    ob_ref[0, rows[1], :] = jnp.concatenate(outs[N_HEADS:], axis=1).astype(_BF16)


def _scalar_gates(sf, sb, n_sub, dn_gp, ml_gp, dn_gc, dn_gr, ml_gc, ml_gr):
    c = CHUNK
    n_g = 2 * N_HEADS
    tiles = [(d, jb) for d in range(2) for jb in range(n_sub)]
    smts = [(sf, sb)[d][0, jb * c:(jb + 1) * c, :].T for d, jb in tiles]
    betas = [_sigmoid(t[SM_BETA:SM_BETA + n_g]) for t in smts]
    gs = [-jnp.exp(dn_gp[n_g:2 * n_g]) * _softplus(t[SM_ALPHA:SM_ALPHA + n_g] + dn_gp[0:n_g]) for t in smts]
    igs = [t[SM_IG:SM_IG + n_g] + ml_gp[0:n_g] for t in smts]
    fts = [t[SM_FG:SM_FG + n_g] + ml_gp[n_g:2 * n_g] for t in smts]
    logfs = [jnp.minimum(f, 0.0) - jnp.log(1.0 + jnp.exp(-jnp.abs(f))) for f in fts]
    cums = [_cumsum_lanes(jnp.concatenate(gs[d * n_sub:(d + 1) * n_sub] + logfs[d * n_sub:(d + 1) * n_sub], axis=0),
                          d == 1) for d in range(2)]
    gcums = [cums[d][jb * n_g:(jb + 1) * n_g] for d, jb in tiles]
    bcums = [cums[d][(n_sub + jb) * n_g:(n_sub + jb + 1) * n_g] for d, jb in tiles]
    gtots = [jnp.sum(g, axis=1, keepdims=True) for g in gs]
    btots = [jnp.sum(f, axis=1, keepdims=True) for f in logfs]
    pad = lambda n: jnp.zeros((HEAD_DIM - n * n_g, c), _F32)
    dn_cols = [jnp.concatenate([betas[i], gcums[i], jnp.exp(gcums[i]), jnp.exp(gtots[i] - gcums[i]), pad(4)],
                               axis=0).T for i in range(len(tiles))]
    ml_cols = [jnp.concatenate([igs[i], bcums[i], pad(2)], axis=0).T for i in range(len(tiles))]
    for i, (d, jb) in enumerate(tiles):
        dn_gc[d, jb] = dn_cols[i]
        dn_gr[d, jb] = jnp.concatenate([gcums[i], jnp.broadcast_to(jnp.exp(gtots[i]), (n_g, c))], axis=0)
        ml_gc[d, jb] = ml_cols[i]
        ml_gr[d, jb] = jnp.concatenate([igs[i], bcums[i], jnp.broadcast_to(btots[i], (n_g, c))], axis=0)


def _mix_kernel(sf, sb, dn_af, dn_ab, dn_gp, hg_qf, hg_ff, hg_vf, hg_qb, hg_fb, hg_vb, hg_lb,
                ml_af, ml_vf, ml_ab, ml_vb, ml_gp,
                dn_of, dn_ob, hg_of, hg_ob, ml_of, ml_ob,
                dn_s, hg_s, ml_c, ml_m, dn_gc, dn_gr, ml_gc, ml_gr):
    @pl.when(pl.program_id(1) == 0)
    def _():
        dn_s[...] = jnp.zeros_like(dn_s)
        hg_s[...] = jnp.zeros_like(hg_s)
        ml_c[...] = jnp.zeros_like(ml_c)
        ml_m[...] = jnp.full(ml_m.shape, NEG_BIG, _F32)

    n_sub = sf.shape[1] // CHUNK
    _scalar_gates(sf, sb, n_sub, dn_gp, ml_gp, dn_gc, dn_gr, ml_gc, ml_gr)

    def chunk_step(j, carry):
        jb = (j, n_sub - 1 - j)
        rows = tuple(pl.ds(pl.multiple_of(i * CHUNK, CHUNK), CHUNK) for i in jb)
        dn = _dn_body(rows, dn_af, dn_ab, [(dn_gc[d, jb[d]], dn_gr[d, jb[d]]) for d in range(2)],
                      dn_of, dn_ob, dn_s)
        hg = _hg_body(rows, hg_qf, hg_ff, hg_vf, hg_qb, hg_fb, hg_vb, hg_lb, hg_of, hg_ob, hg_s)
        ml = _ml_body(rows, ml_af, ml_vf, ml_ab, ml_vb, [(ml_gc[d, jb[d]], ml_gr[d, jb[d]]) for d in range(2)],
                      ml_of, ml_ob, ml_c, ml_m)
        pending = [dn, hg, dn, ml]
        while pending:
            for gen in list(pending):
                if gen in pending and next(gen, _DONE) is _DONE:
                    pending = [g for g in pending if g is not gen]
        return carry

    lax.fori_loop(0, n_sub, chunk_step, 0)


def _mix_call(zb3, zf3, dn_act, ml_act, dn_gates, hg_lb, ml_gates):
    b, s, _ = zb3.shape
    c = min(MIX_ROWS, s)
    n_chunks = s // c
    n_sub = c // CHUNK
    pair = lambda width, off: _pair_specs(c, n_chunks, width, off)
    sm, dn_a, ml_a = pair(HEAD_DIM, SM_OFF), pair(DN_ACT_W, 0), pair(ML_ACT_W, 0)
    hg_q, hg_v, ml_v = pair(MIX_WIDTH, HG_Q_OFF), pair(MIX_WIDTH, HG_I_OFF), pair(MIX_WIDTH, ML_V_OFF)
    hg_ff = pair(MIX_WIDTH, HG_F_OFF)[0]
    hg_fb = pair(MIX_WIDTH, HG_F_OFF + MIX_WIDTH)[1]
    in_specs = [sm[0], sm[1], dn_a[0], dn_a[1], _const_spec(dn_gates.shape),
                hg_q[0], hg_ff, hg_v[0], hg_q[1], hg_fb, hg_v[1], _const_spec(hg_lb.shape),
                ml_a[0], ml_v[0], ml_a[1], ml_v[1], _const_spec(ml_gates.shape)]
    args = (zf3, zf3, dn_act, dn_act, dn_gates, zb3, zf3, zb3, zb3, zf3, zb3, hg_lb,
            ml_act, zb3, ml_act, zb3, ml_gates)
    out = jax.ShapeDtypeStruct((b, s, MIX_WIDTH), _BF16)
    state = (2, N_HEADS, HEAD_DIM, HEAD_DIM)
    return pl.pallas_call(
        _mix_kernel,
        grid=(b, n_chunks),
        in_specs=in_specs,
        out_specs=pair(MIX_WIDTH, 0) * 3,
        out_shape=[out] * 6,
        scratch_shapes=[pltpu.VMEM(state, _F32), pltpu.VMEM(state, _F32),
                        pltpu.VMEM((2, N_HEADS, HEAD_DIM, 2 * HEAD_DIM), _F32),
                        pltpu.VMEM((2 * N_HEADS, HEAD_DIM), _F32),
                        pltpu.VMEM((2, n_sub, CHUNK, HEAD_DIM), _F32), pltpu.VMEM((2, n_sub, 4 * N_HEADS, CHUNK), _F32),
                        pltpu.VMEM((2, n_sub, CHUNK, HEAD_DIM), _F32), pltpu.VMEM((2, n_sub, 6 * N_HEADS, CHUNK), _F32)],
        compiler_params=pltpu.CompilerParams(dimension_semantics=("arbitrary", "arbitrary"),
                                             vmem_limit_bytes=VMEM_LIMIT),
        name="mixers",
    )(*args)


def _head_norm(o, w):
    parts = []
    for h in range(N_HEADS):
        seg = _head(o, h)
        parts.append(seg * lax.rsqrt(jnp.mean(seg * seg, axis=-1, keepdims=True) + EPS))
    return jnp.concatenate(parts, axis=1) * w


def _merge_kernel(x_ref, dnf, dnb, hgf, hgb, mlf, mlb, dnz, hgz, mlo, gp_ref,
                  nw_ref, wbr_ref, wout_ref, gpost_ref, o_ref):
    both = lambda f, b: f[...].astype(_F32) + b[...].astype(_F32)
    branches = (
        _head_norm(both(dnf, dnb), nw_ref[0:1, :]) * _silu_of_half(dnz[...].astype(_F32)),
        _head_norm(both(hgf, hgb), nw_ref[1:2, :]) * _silu_of_half(hgz[...].astype(_F32)),
        _head_norm(both(mlf, mlb), nw_ref[2:3, :]) * _two_sigmoid_of_half(mlo[...].astype(_F32)),
    )
    merged = None
    for i, br in enumerate(branches):
        gate = _two_sigmoid_of_half(gp_ref[:, i * D_MODEL:(i + 1) * D_MODEL].astype(_F32))
        term = gate * _bdot(br, wbr_ref[i])
        merged = term if merged is None else merged + term
    y = _bdot(merged, wout_ref[...])
    o_ref[...] = x_ref[...] + _rmsnorm(y, gpost_ref[...])


def _merge_call(x2, zb2, mix_outs, norm_w, w_branch, w_out, g_post):
    t = x2.shape[0]
    tm = min(MERGE_TM, t)
    row = lambda w: pl.BlockSpec((tm, w), lambda i: (i, 0))
    zblk = lambda w, off: pl.BlockSpec((tm, w), lambda i: (i, off // w))
    full = lambda a: pl.BlockSpec(a.shape, lambda i: (0,) * a.ndim)
    in_specs = ([row(D_MODEL)] + [row(MIX_WIDTH)] * 6
                + [zblk(MIX_WIDTH, DN_Z_OFF), zblk(MIX_WIDTH, HG_Z_OFF), zblk(MIX_WIDTH, ML_O_OFF),
                   zblk(N_BRANCH * D_MODEL, GP_OFF)]
                + [full(norm_w), full(w_branch), full(w_out), full(g_post)])
    return pl.pallas_call(
        _merge_kernel,
        grid=(t // tm,),
        in_specs=in_specs,
        out_specs=row(D_MODEL),
        out_shape=jax.ShapeDtypeStruct((t, D_MODEL), _F32),
        compiler_params=pltpu.CompilerParams(dimension_semantics=("arbitrary",),
                                             vmem_limit_bytes=VMEM_LIMIT),
        name="merge",
    )(x2, *mix_outs, zb2, zb2, zb2, zb2, norm_w, w_branch, w_out, g_post)


def _ffn_kernel(x_ref, gpre_ref, w1_ref, w2_ref, gpost_ref, o_ref):
    x = x_ref[...]
    h = _rmsnorm(x, gpre_ref[...]).astype(_BF16)
    y = None
    for s in range(D_FF // FFN_SLAB):
        cols = slice(s * FFN_SLAB, (s + 1) * FFN_SLAB)
        u = jnp.maximum(jnp.dot(h, w1_ref[:, cols], preferred_element_type=_F32), 0.0)
        part = _bdot(u * u, w2_ref[cols, :])
        y = part if y is None else y + part
    o_ref[...] = x + _rmsnorm(y, gpost_ref[...])


def _ffn_call(x2, g_pre, w1, w2, g_post):
    t = x2.shape[0]
    tm = min(FFN_TM, t)
    row = pl.BlockSpec((tm, D_MODEL), lambda i: (i, 0))
    full = lambda a: pl.BlockSpec(a.shape, lambda i: (0,) * a.ndim, pipeline_mode=pl.Buffered(1))
    return pl.pallas_call(
        _ffn_kernel,
        grid=(t // tm,),
        in_specs=[row, full(g_pre), full(w1), full(w2), full(g_post)],
        out_specs=row,
        out_shape=jax.ShapeDtypeStruct((t, D_MODEL), _F32),
        compiler_params=pltpu.CompilerParams(dimension_semantics=("arbitrary",),
                                             vmem_limit_bytes=VMEM_LIMIT),
        name="ffn",
    )(x2, g_pre, w1, w2, g_post)


def _permute_w_in(w_in):
    m = MIX_WIDTH
    o_dn, o_sm_dn, o_hg, o_ml, o_sm_ml, o_gp = 0, 4 * m, 4 * m + 16, 9 * m + 16, 13 * m + 16, 13 * m + 32
    wide = [
        0.5 * w_in[:, o_gp:o_gp + N_BRANCH * D_MODEL],
        w_in[:, o_dn:o_dn + 3 * m],
        0.5 * w_in[:, o_dn + 3 * m:o_dn + 4 * m],
        w_in[:, o_ml:o_ml + 3 * m],
        0.5 * w_in[:, o_ml + 3 * m:o_ml + 4 * m],
        0.5 * w_in[:, o_hg:o_hg + m],
        w_in[:, o_hg + 3 * m:o_hg + 4 * m],
        0.5 * w_in[:, o_hg + 4 * m:o_hg + 5 * m],
        jnp.zeros((w_in.shape[0], NB_COLS - HG_Z_OFF - m), w_in.dtype),
    ]
    gates = [
        w_in[:, o_hg + m:o_hg + 3 * m],
        w_in[:, o_sm_dn:o_sm_dn + 16],
        w_in[:, o_sm_ml:o_sm_ml + 16],
        jnp.zeros((w_in.shape[0], NF_COLS - SM_OFF - 32), w_in.dtype),
    ]
    return jnp.concatenate(wide, axis=1).astype(_BF16), jnp.concatenate(gates, axis=1).astype(_BF16)


def _gate_rows(first, second):
    vals = jnp.concatenate([first.astype(_F32).reshape(-1), second.astype(_F32).reshape(-1)])
    return jnp.broadcast_to(vals[:, None], (vals.shape[0], HEAD_DIM))


def _layer(x3, p):
    b, s, _ = x3.shape
    x2 = x3.reshape(b * s, D_MODEL)
    w_wide, w_gates = p["w_in"]
    zb2, zf2 = _proj(x2, p["norm_mix_pre"], w_wide, w_gates)
    zb3 = zb2.reshape(b, s, NB_COLS)
    zf3 = zf2.reshape(b, s, NF_COLS)
    dn_act, ml_act = _prep_call(zb3, p["dn_conv"], p["ml_conv"])
    mix = _mix_call(zb3, zf3, dn_act, ml_act, p["dn_gates"], p["hg_lb"], p["ml_gates"])
    mix = [a.reshape(b * s, MIX_WIDTH) for a in mix]
    x2 = _merge_call(x2, zb2, mix, p["mix_norm"], p["w_branch"], p["w_out"], p["norm_mix_post"])
    x2 = _ffn_call(x2, p["norm_ffn_pre"], p["w_ff1"], p["w_ff2"], p["norm_ffn_post"])
    return x2.reshape(b, s, D_MODEL)


def _layer_params(l, lower_bounds, norm_mix_pre, norm_mix_post, norm_ffn_pre, norm_ffn_post, w_in, dn_conv,
                  dn_a_log, dn_dt_bias, dn_norm, hg_norm, ml_conv, ml_i_bias, ml_f_bias, ml_norm,
                  w_branch, w_out, w_ff1, w_ff2):
    return {
        "norm_mix_pre": norm_mix_pre[l].reshape(1, D_MODEL),
        "norm_mix_post": norm_mix_post[l].reshape(1, D_MODEL),
        "norm_ffn_pre": norm_ffn_pre[l].reshape(1, D_MODEL),
        "norm_ffn_post": norm_ffn_post[l].reshape(1, D_MODEL),
        "w_in": _permute_w_in(w_in[l]),
        "dn_conv": 0.5 * dn_conv[l],
        "dn_gates": _gate_rows(dn_dt_bias[l], dn_a_log[l]),
        "hg_lb": lower_bounds[l],
        "ml_conv": 0.5 * ml_conv[l],
        "ml_gates": _gate_rows(ml_i_bias[l], ml_f_bias[l]),
        "mix_norm": jnp.stack([dn_norm[l].reshape(-1), hg_norm[l].reshape(-1), 0.5 * ml_norm[l].reshape(-1)]
                              + [jnp.zeros((MIX_WIDTH,), _F32)] * 5),
        "w_branch": (0.5 * w_branch[l]).astype(_BF16),
        "w_out": w_out[l].astype(_BF16),
        "w_ff1": w_ff1[l].astype(_BF16),
        "w_ff2": w_ff2[l].astype(_BF16),
    }


def kernel(x_prompt, x_sample, norm_mix_pre, norm_mix_post, norm_ffn_pre, norm_ffn_post, w_in, dn_conv,
           dn_a_log, dn_dt_bias, dn_norm, hg_lb, hg_norm, ml_conv, ml_i_bias, ml_f_bias, ml_norm,
           w_branch, w_out, w_ff1, w_ff2):
    p_layers = jax.nn.softmax(hg_lb.astype(_F32), axis=0)
    lower_bounds = jnp.cumsum(p_layers, axis=0) - p_layers[0]
    y_prompt, y_sample = x_prompt, x_sample
    for l in range(w_in.shape[0]):
        p = _layer_params(l, lower_bounds, norm_mix_pre, norm_mix_post, norm_ffn_pre, norm_ffn_post, w_in,
                          dn_conv, dn_a_log, dn_dt_bias, dn_norm, hg_norm, ml_conv, ml_i_bias, ml_f_bias,
                          ml_norm, w_branch, w_out, w_ff1, w_ff2)
        y_prompt = _layer(y_prompt, p)
        y_sample = _layer(y_sample, p)
    return (y_prompt, y_sample)
```
